```python
import jax, jax.numpy as jnp
from jax import lax
import numpy as np

D_MODEL = 1024
BATCH = 2
SEQ = 8192
DEPTH = 2

N_META = 16
CHUNK = 64
SUB = 16
RET_HEADS = 4
RET_DK = 128
RET_DV = 128
GLA_HEADS = 4
GLA_DK = 64
GLA_DV = 128
GLA_GATE_RANK = 16
GLA_TAU = 16.0
D_FF = 2816
CONV_W = 3
ROPE_BASE = 10000.0
EPS = 1e-6

RET_QK = RET_HEADS * RET_DK
RET_V = RET_HEADS * RET_DV
GLA_QK = GLA_HEADS * GLA_DK
GLA_V = GLA_HEADS * GLA_DV
D_MIX = RET_V + GLA_V
IN_SPLITS = (RET_QK, RET_QK, RET_V, RET_V, GLA_QK, GLA_QK, GLA_V, GLA_V, GLA_GATE_RANK)
IN_WIDTH = 2 * RET_QK + 2 * RET_V + 2 * GLA_QK + 2 * GLA_V + GLA_GATE_RANK

kernel_name = "hybrid_retention_gla_convffn"


def _rmsnorm(x, w):
    xf = x.astype(jnp.float32)
    y = xf * lax.rsqrt(jnp.mean(xf * xf, axis=-1, keepdims=True) + EPS)
    return (y * w.astype(jnp.float32)).astype(x.dtype)


def _rope(t, pos):
    half = t.shape[-1] // 2
    inv = ROPE_BASE ** (-jnp.arange(half, dtype=jnp.float32) / half)
    ang = pos[:, None] * inv[None, :]
    c = jnp.cos(ang)[None, :, None, :]
    s = jnp.sin(ang)[None, :, None, :]
    t = t.astype(jnp.float32)
    t1, t2 = t[..., :half], t[..., half:]
    return jnp.concatenate([t1 * c - t2 * s, t1 * s + t2 * c], axis=-1)


def _to_chunks(t):
    t = jnp.pad(t.astype(jnp.float32), ((0, 0), (CHUNK - N_META, 0), (0, 0), (0, 0)))
    b, lp, h, d = t.shape
    return t.reshape(b, lp // CHUNK, CHUNK, h, d).transpose(0, 3, 1, 2, 4)


def _from_chunks(o):
    b, h, n, c, d = o.shape
    o = o.transpose(0, 2, 3, 1, 4).reshape(b, n * c, h, d)
    return o[:, CHUNK - N_META:]


def _scan_states(decay, kv):
    def step(state, inp):
        dec_n, kv_n = inp
        return dec_n * state + kv_n, state
    init = jnp.zeros(kv.shape[:2] + kv.shape[3:], kv.dtype)
    _, prev = lax.scan(step, init, (jnp.moveaxis(decay, 2, 0), jnp.moveaxis(kv, 2, 0)))
    return jnp.moveaxis(prev, 0, 2)


def _retention(q, k, v):
    b, h, n, c, _ = q.shape
    log_g = jnp.log(1.0 - 2.0 ** (-5.0 - jnp.arange(h, dtype=jnp.float32)))
    idx = jnp.arange(c, dtype=jnp.float32)
    diff = idx[:, None] - idx[None, :]
    dmat = jnp.where(diff >= 0, jnp.exp(log_g[:, None, None] * jnp.maximum(diff, 0.0)), 0.0)
    k = k * (RET_DK ** -0.5)
    scores = jnp.einsum('bhncd,bhnsd->bhncs', q, k) * dmat[None, :, None]
    o_intra = jnp.einsum('bhncs,bhnsv->bhncv', scores, v)
    zeta = jnp.exp(log_g[:, None] * (c - 1.0 - idx)[None, :])
    kv = jnp.einsum('bhncd,hc,bhncv->bhndv', k, zeta, v)
    chunk_decay = jnp.broadcast_to(jnp.exp(log_g * c)[None, :, None, None, None], (b, h, n, 1, 1))
    prev = _scan_states(chunk_decay, kv)
    xi = jnp.exp(log_g[:, None] * (idx + 1.0)[None, :])
    o_inter = jnp.einsum('bhncd,bhndv->bhncv', q, prev) * xi[None, :, None, :, None]
    return o_intra + o_inter


def _gla(q, k, v, log_a):
    b, h, n, c, dk = q.shape
    dv = v.shape[-1]
    ns = c // SUB
    q = q * (GLA_DK ** -0.5)
    cum = jnp.cumsum(log_a, axis=3)
    last = cum[:, :, :, -1:, :]
    kv = jnp.einsum('bhncd,bhncv->bhndv', k * jnp.exp(last - cum), v)
    prev = _scan_states(jnp.exp(last[:, :, :, 0, :])[..., None], kv)
    o_inter = jnp.einsum('bhncd,bhndv->bhncv', q * jnp.exp(cum), prev)
    qs = q.reshape(b, h, n, ns, SUB, dk)
    ks = k.reshape(b, h, n, ns, SUB, dk)
    vs = v.reshape(b, h, n, ns, SUB, dv)
    cs = cum.reshape(b, h, n, ns, SUB, dk)
    ref = jnp.concatenate([jnp.zeros((b, h, n, 1, dk), cum.dtype), cum[:, :, :, SUB - 1:c - 1:SUB, :]], axis=3)
    q_hat = qs * jnp.exp(cs - ref[:, :, :, :, None, :])
    k_hat = k[:, :, :, None] * jnp.exp(jnp.minimum(ref[:, :, :, :, None, :] - cum[:, :, :, None], 0.0))
    off = jnp.einsum('bhnasd,bhnacd->bhnasc', q_hat, k_hat)
    off_mask = jnp.arange(c)[None, :] < (jnp.arange(ns) * SUB)[:, None]
    off = jnp.where(off_mask[:, None, :], off, 0.0)
    o_off = jnp.einsum('bhnasc,bhncv->bhnasv', off, v)
    causal = jnp.tril(jnp.ones((SUB, SUB), dtype=bool))
    ddiff = cs[..., :, None, :] - cs[..., None, :, :]
    dec = jnp.exp(jnp.where(causal[:, :, None], ddiff, -jnp.inf))
    diag = jnp.einsum('bhnasd,bhnatd,bhnastd->bhnast', qs, ks, dec)
    o_diag = jnp.einsum('bhnast,bhnatv->bhnasv', diag, vs)
    return o_inter + (o_off + o_diag).reshape(b, h, n, c, dv)


def _head_group_norm(o, w):
    mu = jnp.mean(o, axis=-1, keepdims=True)
    var = jnp.mean(jnp.square(o - mu), axis=-1, keepdims=True)
    y = (o - mu) * lax.rsqrt(var + EPS)
    return y.reshape(o.shape[0], o.shape[1], -1) * w.astype(jnp.float32)


def _head_rms_norm(o, w):
    y = o * lax.rsqrt(jnp.mean(o * o, axis=-1, keepdims=True) + EPS)
    return y.reshape(o.shape[0], o.shape[1], -1) * w.astype(jnp.float32)


def _mixer(h, pos, w_in, gla_gate_w2, gla_gate_b, ret_norm_w, gla_norm_w, w_out):
    bsz, length, _ = h.shape
    proj = h @ w_in
    offs = np.cumsum(np.array(IN_SPLITS))[:-1].tolist()
    rq, rk, rv, rg, gq, gk, gv, gr, ga = jnp.split(proj, offs, axis=-1)
    rq = _rope(rq.reshape(bsz, length, RET_HEADS, RET_DK), pos)
    rk = _rope(rk.reshape(bsz, length, RET_HEADS, RET_DK), pos)
    rv = rv.reshape(bsz, length, RET_HEADS, RET_DV)
    o_ret = _from_chunks(_retention(_to_chunks(rq), _to_chunks(rk), _to_chunks(rv)))
    o_ret = _head_group_norm(o_ret, ret_norm_w) * jax.nn.silu(rg.astype(jnp.float32))
    z = (ga @ gla_gate_w2 + gla_gate_b).astype(jnp.float32)
    log_a = (jax.nn.log_sigmoid(z) / GLA_TAU).reshape(bsz, length, GLA_HEADS, GLA_DK)
    gq = gq.reshape(bsz, length, GLA_HEADS, GLA_DK)
    gk = gk.reshape(bsz, length, GLA_HEADS, GLA_DK)
    gv = gv.reshape(bsz, length, GLA_HEADS, GLA_DV)
    o_gla = _from_chunks(_gla(_to_chunks(gq), _to_chunks(gk), _to_chunks(gv), _to_chunks(log_a)))
    o_gla = _head_rms_norm(o_gla, gla_norm_w) * jax.nn.silu(gr.astype(jnp.float32))
    merged = jnp.concatenate([o_ret, o_gla], axis=-1).astype(h.dtype)
    return merged @ w_out


def _conv_ffn(h, ffn_up, ffn_conv_w, ffn_conv_b, ffn_down):
    length = h.shape[1]
    u = h @ ffn_up
    up = jnp.pad(u, ((0, 0), (CONV_W - 1, 0), (0, 0)))
    conv = ffn_conv_b + sum(up[:, i:i + length] * ffn_conv_w[i] for i in range(CONV_W))
    a, g = jnp.split(conv, 2, axis=-1)
    return (jax.nn.gelu(a, approximate=True) * g) @ ffn_down


def setup_inputs(seed: int = 0) -> dict:
    key = jax.random.key(seed)
    ks = jax.random.split(key, 18)
    nrm = lambda k, shape, s: jax.random.normal(k, shape, jnp.float32) * s
    gain = lambda k, shape: 1.0 + 0.02 * jax.random.normal(k, shape, jnp.float32)
    return {
        "x": nrm(ks[0], (BATCH, SEQ, D_MODEL), 1.0),
        "meta_tokens": nrm(ks[1], (N_META, D_MODEL), 1.0),
        "pre_mix_norm": gain(ks[2], (DEPTH, D_MODEL)),
        "w_in": nrm(ks[3], (DEPTH, D_MODEL, IN_WIDTH), D_MODEL ** -0.5),
        "gla_gate_w2": nrm(ks[4], (DEPTH, GLA_GATE_RANK, GLA_QK), GLA_GATE_RANK ** -0.5),
        "gla_gate_b": nrm(ks[5], (DEPTH, GLA_QK), 0.1),
        "ret_norm_w": gain(ks[6], (DEPTH, RET_V)),
        "gla_norm_w": gain(ks[7], (DEPTH, GLA_V)),
        "w_out": nrm(ks[8], (DEPTH, D_MIX, D_MODEL), D_MIX ** -0.5),
        "post_mix_norm": gain(ks[9], (DEPTH, D_MODEL)),
        "pre_ffn_norm": gain(ks[10], (DEPTH, D_MODEL)),
        "ffn_up": nrm(ks[11], (DEPTH, D_MODEL, 2 * D_FF), D_MODEL ** -0.5),
        "ffn_conv_w": nrm(ks[12], (DEPTH, CONV_W, 2 * D_FF), CONV_W ** -0.5),
        "ffn_conv_b": nrm(ks[13], (DEPTH, 2 * D_FF), 0.02),
        "ffn_down": nrm(ks[14], (DEPTH, D_FF, D_MODEL), D_FF ** -0.5),
        "post_ffn_norm": gain(ks[15], (DEPTH, D_MODEL)),
    }


def reference(x, meta_tokens, pre_mix_norm, w_in, gla_gate_w2, gla_gate_b, ret_norm_w, gla_norm_w,
              w_out, post_mix_norm, pre_ffn_norm, ffn_up, ffn_conv_w, ffn_conv_b, ffn_down, post_ffn_norm):
    bsz = x.shape[0]
    meta = jnp.broadcast_to(meta_tokens.astype(x.dtype)[None], (bsz, N_META, x.shape[-1]))
    h = jnp.concatenate([meta, x], axis=1)
    pos = jnp.arange(h.shape[1], dtype=jnp.float32)
    for i in range(DEPTH):
        m = _mixer(_rmsnorm(h, pre_mix_norm[i]), pos, w_in[i], gla_gate_w2[i], gla_gate_b[i],
                   ret_norm_w[i], gla_norm_w[i], w_out[i])
        h = h + _rmsnorm(m, post_mix_norm[i])
        f = _conv_ffn(_rmsnorm(h, pre_ffn_norm[i]), ffn_up[i], ffn_conv_w[i], ffn_conv_b[i], ffn_down[i])
        h = h + _rmsnorm(f, post_ffn_norm[i])
    return h[:, N_META:]
```

```python
import functools

import numpy as np
import jax
import jax.numpy as jnp
from jax import lax
from jax.experimental import pallas as pl
from jax.experimental.pallas import tpu as pltpu

D_MODEL = 1024
N_META = 16
RET_HEADS = 4
RET_DK = 128
RET_DV = 128
GLA_HEADS = 4
GLA_DK = 64
GLA_DV = 128
GLA_GATE_RANK = 16
GLA_TAU = 16.0
D_FF = 2816
CONV_W = 3
ROPE_BASE = 10000.0
EPS = 1e-6

RET_QK = RET_HEADS * RET_DK
RET_V = RET_HEADS * RET_DV
GLA_QK = GLA_HEADS * GLA_DK
GLA_V = GLA_HEADS * GLA_DV
D_MIX = RET_V + GLA_V
IN_WIDTH = 2 * RET_QK + 2 * RET_V + 2 * GLA_QK + 2 * GLA_V + GLA_GATE_RANK

LANES = 128
CH = 128
PAD = CH - N_META
IN_PAD = 3712
GA_COL = IN_WIDTH - GLA_GATE_RANK
FF_TILE = 256
VMEM_LIMIT = 56 * 1024 * 1024

BF16 = jnp.bfloat16
F32 = jnp.float32

_NT = (((1,), (1,)), ((), ()))
_TN = (((0,), (0,)), ((), ()))


def _rms(x):
    return x * lax.rsqrt(jnp.mean(x * x, axis=-1, keepdims=True) + EPS)


def _silu(g):
    return g * (1.0 / (1.0 + jnp.exp(-g)))


def _const_spec(shape):
    nd = len(shape)
    return pl.BlockSpec(shape, lambda *_: (0,) * nd, pipeline_mode=pl.Buffered(1))


def _proj_kernel(h_ref, nw_ref, w_ref, out_ref):
    hn = (_rms(h_ref[0]) * nw_ref[...]).astype(BF16)
    for c0 in range(0, IN_PAD, 512):
        cw = min(512, IN_PAD - c0)
        out_ref[0, :, c0:c0 + cw] = jnp.dot(hn, w_ref[:, c0:c0 + cw], preferred_element_type=F32)


def _proj(h, nw, w, tm):
    b, lp, d = h.shape
    return pl.pallas_call(
        _proj_kernel,
        grid=(b, lp // tm),
        in_specs=[pl.BlockSpec((1, tm, d), lambda i, j: (i, j, 0)),
                  _const_spec((1, d)),
                  _const_spec((d, IN_PAD))],
        out_specs=pl.BlockSpec((1, tm, IN_PAD), lambda i, j: (i, j, 0)),
        out_shape=jax.ShapeDtypeStruct((b, lp, IN_PAD), F32),
        compiler_params=pltpu.CompilerParams(dimension_semantics=("parallel", "parallel"),
                                             vmem_limit_bytes=VMEM_LIMIT),
        name="proj",
    )(h, nw, w)


def _mixer_constants():
    c = CH
    idx = np.arange(c, dtype=np.float64)
    gam = 1.0 - 2.0 ** (-5.0 - np.arange(RET_HEADS, dtype=np.float64))
    diff = idx[:, None] - idx[None, :]
    dmat = np.where(diff >= 0, gam[:, None, None] ** np.maximum(diff, 0.0), 0.0)
    zeta = np.repeat((gam[:, None] ** (c - 1.0 - idx)[None, :]).T, RET_DK, axis=1)
    xi = np.repeat((gam[:, None] ** (idx + 1.0)[None, :]).T, RET_DK, axis=1)
    gch = tuple(float(g ** c) for g in gam)
    s = np.arange(c)
    wl = [np.tril(np.ones((c, c)))]
    ml = [np.eye(c)]
    m = c // 2
    while m >= 1:
        blk = s // (2 * m)
        upper = (s % (2 * m)) >= m
        refpos = blk * 2 * m + m - 1
        r = s[None, :]
        w_up = (r > refpos[:, None]) & (r <= s[:, None])
        w_lo = (r > s[:, None]) & (r <= refpos[:, None])
        wl.append(np.where(upper[:, None], w_up, w_lo).astype(np.float64))
        ml.append(((blk[:, None] == blk[None, :]) & upper[:, None] & (~upper)[None, :]).astype(np.float64))
        m //= 2
    wlev = np.concatenate(wl, axis=0)
    mlev = np.stack(ml, axis=0)
    return (jnp.asarray(dmat, F32), jnp.asarray(zeta, F32), jnp.asarray(xi, F32), gch,
            jnp.asarray(wlev, BF16), jnp.asarray(mlev, F32))


def _mixer_kernel(gch, nlev,
                  rq_ref, rk_ref, rv_ref, rg_ref, gq_ref, gk_ref, gv_ref, gr_ref, ga_ref,
                  cos_ref, sin_ref, w2_ref, gb_ref, rnw_ref, gnw_ref,
                  dmat_ref, zeta_ref, xi_ref, wlev_ref, mlev_ref,
                  out_ref, sret_ref, sgla_ref):
    j = pl.program_id(1)

    @pl.when(j == 0)
    def _():
        sret_ref[...] = jnp.zeros_like(sret_ref)
        sgla_ref[...] = jnp.zeros_like(sgla_ref)

    cos = cos_ref[...]
    sin = sin_ref[...]
    for h in range(RET_HEADS):
        sl = slice(h * RET_DK, (h + 1) * RET_DK)
        q = rq_ref[0, :, sl]
        k = rk_ref[0, :, sl]
        q = q * cos + pltpu.roll(q, RET_DK // 2, 1) * sin
        k = (k * cos + pltpu.roll(k, RET_DK // 2, 1) * sin) * (RET_DK ** -0.5)
        vb = rv_ref[0, :, sl].astype(BF16)
        sc = lax.dot_general(q.astype(BF16), k.astype(BF16), _NT, preferred_element_type=F32) * dmat_ref[h]
        st = sret_ref[h]
        o = (jnp.dot(sc.astype(BF16), vb, preferred_element_type=F32)
             + jnp.dot((q * xi_ref[:, sl]).astype(BF16), st.astype(BF16), preferred_element_type=F32))
        kz = (k * zeta_ref[:, sl]).astype(BF16)
        sret_ref[h] = st * gch[h] + lax.dot_general(kz, vb, _TN, preferred_element_type=F32)
        d = o - jnp.mean(o, axis=-1, keepdims=True)
        y = d * lax.rsqrt(jnp.mean(d * d, axis=-1, keepdims=True) + EPS) * rnw_ref[:, sl]
        out_ref[0, :, sl] = (y * _silu(rg_ref[0, :, sl])).astype(out_ref.dtype)

    z = jnp.dot(ga_ref[0], w2_ref[...], precision=lax.Precision.HIGHEST,
                preferred_element_type=F32) + gb_ref[...]
    la = (jnp.minimum(z, 0.0) - jnp.log1p(jnp.exp(-jnp.abs(z)))) * (1.0 / GLA_TAU)
    row = j * CH + lax.broadcasted_iota(jnp.int32, (CH, 1), 0)
    la = jnp.where(row >= PAD, la, 0.0)
    la_hi = la.astype(BF16)
    r1 = la - la_hi.astype(F32)
    la_mid = r1.astype(BF16)
    la_lo = (r1 - la_mid.astype(F32)).astype(BF16)
    wlev = wlev_ref[...]
    ex = (jnp.dot(wlev, la_hi, preferred_element_type=F32)
          + jnp.dot(wlev, la_mid, preferred_element_type=F32)
          + jnp.dot(wlev, la_lo, preferred_element_type=F32))
    cum = ex[:CH]
    last = cum[CH - 1:CH, :]

    lane = lax.broadcasted_iota(jnp.int32, (1, GLA_QK), 1)
    even = (lane % LANES) < GLA_DK
    qg = gq_ref[0] * (GLA_DK ** -0.5)
    kg = gk_ref[0]
    qg_e = jnp.where(even, qg, 0.0)
    qg_o = jnp.where(even, 0.0, qg)

    amat = [None] * GLA_HEADS
    for lv in range(nlev + 1):
        if lv == 0:
            qe, qo, kh = qg_e.astype(BF16), qg_o.astype(BF16), kg.astype(BF16)
        else:
            e = jnp.exp(ex[lv * CH:(lv + 1) * CH])
            qe, qo, kh = (qg_e * e).astype(BF16), (qg_o * e).astype(BF16), (kg * e).astype(BF16)
        msk = mlev_ref[lv]
        for p in range(GLA_HEADS // 2):
            pls = slice(p * LANES, (p + 1) * LANES)
            lhs = jnp.concatenate([qe[:, pls], qo[:, pls]], axis=0)
            sc = lax.dot_general(lhs, kh[:, pls], _NT, preferred_element_type=F32)
            for t in range(2):
                term = sc[t * CH:(t + 1) * CH] * msk
                amat[2 * p + t] = term if lv == 0 else amat[2 * p + t] + term

    ecum = jnp.exp(cum)
    qd_e = (qg_e * ecum).astype(BF16)
    qd_o = (qg_o * ecum).astype(BF16)
    kd = kg * jnp.exp(last - cum)
    kd_e = jnp.where(even, kd, 0.0).astype(BF16)
    kd_o = jnp.where(even, 0.0, kd).astype(BF16)
    dl = jnp.exp(last)
    for p in range(GLA_HEADS // 2):
        pls = slice(p * LANES, (p + 1) * LANES)
        st = sgla_ref[:, pls]
        stb = st.astype(BF16)
        vbs = []
        for t in range(2):
            h = 2 * p + t
            sl = slice(h * GLA_DV, (h + 1) * GLA_DV)
            vb = gv_ref[0, :, sl].astype(BF16)
            vbs.append(vb)
            qd = (qd_e, qd_o)[t][:, pls]
            o = (jnp.dot(amat[h].astype(BF16), vb, preferred_element_type=F32)
                 + lax.dot_general(qd, stb, _NT, preferred_element_type=F32))
            y = o * lax.rsqrt(jnp.mean(o * o, axis=-1, keepdims=True) + EPS) * gnw_ref[:, sl]
            out_ref[0, :, RET_V + h * GLA_DV:RET_V + (h + 1) * GLA_DV] = (
                y * _silu(gr_ref[0, :, sl])).astype(out_ref.dtype)
        kvt = lax.dot_general(jnp.concatenate(vbs, axis=0),
                              jnp.concatenate([kd_e[:, pls], kd_o[:, pls]], axis=0),
                              _TN, preferred_element_type=F32)
        sgla_ref[:, pls] = st * dl[:, pls] + kvt


def _mixer(proj, cos2, sin2, w2p, gb, rnw, gnw):
    b, lp, _ = proj.shape
    dmat, zeta, xi, gch, wlev, mlev = _mixer_constants()
    nlev = mlev.shape[0] - 1

    def col(width, blk):
        return pl.BlockSpec((1, CH, width), lambda i, j: (i, j, blk))

    in_specs = [
        col(RET_QK, 0), col(RET_QK, 1), col(RET_V, 2), col(RET_V, 3),
        col(GLA_QK, 2048 // GLA_QK), col(GLA_QK, 2304 // GLA_QK),
        col(GLA_V, 2560 // GLA_V), col(GLA_V, 3072 // GLA_V),
        col(LANES, GA_COL // LANES),
        pl.BlockSpec((CH, RET_DK), lambda i, j: (j, 0)),
        pl.BlockSpec((CH, RET_DK), lambda i, j: (j, 0)),
        _const_spec(w2p.shape), _const_spec(gb.shape), _const_spec(rnw.shape), _const_spec(gnw.shape),
        _const_spec(dmat.shape), _const_spec(zeta.shape), _const_spec(xi.shape),
        _const_spec(wlev.shape), _const_spec(mlev.shape),
    ]
    return pl.pallas_call(
        functools.partial(_mixer_kernel, gch, nlev),
        grid=(b, lp // CH),
        in_specs=in_specs,
        out_specs=pl.BlockSpec((1, CH, D_MIX), lambda i, j: (i, j, 0)),
        out_shape=jax.ShapeDtypeStruct((b, lp, D_MIX), BF16),
        scratch_shapes=[pltpu.VMEM((RET_HEADS, RET_DK, RET_DV), F32),
                        pltpu.VMEM((GLA_DV, GLA_QK), F32)],
        compiler_params=pltpu.CompilerParams(dimension_semantics=("parallel", "arbitrary"),
                                             vmem_limit_bytes=VMEM_LIMIT),
        name="mixer",
    )(*([proj] * 9), cos2, sin2, w2p, gb, rnw, gnw, dmat, zeta, xi, wlev, mlev)


def _gelu_tanh(x):
    return 0.5 * x * (1.0 + jnp.tanh(np.sqrt(2.0 / np.pi) * (x + 0.044715 * (x * x * x))))


def _ffn_kernel(tm, m_ref, h_ref, wout_ref, pmn_ref, pfn_ref, up_ref, cw_ref, cb_ref, down_ref, pon_ref,
                out_ref, halo_ref, act_ref):
    i = pl.program_id(1)

    @pl.when(i == 0)
    def _():
        halo_ref[...] = jnp.zeros_like(halo_ref)

    m = jnp.dot(m_ref[0], wout_ref[...], preferred_element_type=F32)
    h1 = h_ref[0] + _rms(m) * pmn_ref[...]
    hn = (_rms(h1) * pfn_ref[...]).astype(BF16)
    sub = lax.broadcasted_iota(jnp.int32, (8, 1), 0)

    def conv(c0):
        cs = slice(c0, c0 + FF_TILE)
        u = jnp.dot(hn, up_ref[:, cs], preferred_element_type=F32)
        prev = halo_ref[:, cs]
        halo_ref[:, cs] = u[tm - 8:tm]
        s1 = pltpu.roll(u, 1, 0)
        s2 = pltpu.roll(u, 2, 0)
        top1 = jnp.where(sub < 1, pltpu.roll(prev, 1, 0), s1[:8])
        top2 = jnp.where(sub < 2, pltpu.roll(prev, 2, 0), s2[:8])
        s1 = jnp.concatenate([top1, s1[8:]], axis=0)
        s2 = jnp.concatenate([top2, s2[8:]], axis=0)
        return (cb_ref[:, cs] + u * cw_ref[2:3, cs] + s1 * cw_ref[1:2, cs] + s2 * cw_ref[0:1, cs])

    for c0 in range(0, D_FF, FF_TILE):
        a = conv(c0)
        g = conv(D_FF + c0)
        act_ref[:, c0:c0 + FF_TILE] = (_gelu_tanh(a) * g).astype(BF16)

    f = jnp.dot(act_ref[...], down_ref[...], preferred_element_type=F32)
    h2 = h1 + _rms(f) * pon_ref[...]
    row = i * tm + lax.broadcasted_iota(jnp.int32, (tm, 1), 0)
    out_ref[0] = jnp.where(row >= PAD, h2, 0.0)


def _ffn(merged, h, wout, pmn, pfn, up, cw, cb, down, pon, tm):
    b, lp, d = h.shape
    row = lambda width: pl.BlockSpec((1, tm, width), lambda i, j: (i, j, 0))
    return pl.pallas_call(
        functools.partial(_ffn_kernel, tm),
        grid=(b, lp // tm),
        in_specs=[row(D_MIX), row(d),
                  _const_spec(wout.shape), _const_spec(pmn.shape), _const_spec(pfn.shape),
                  _const_spec(up.shape), _const_spec(cw.shape), _const_spec(cb.shape),
                  _const_spec(down.shape), _const_spec(pon.shape)],
        out_specs=row(d),
        out_shape=jax.ShapeDtypeStruct((b, lp, d), F32),
        scratch_shapes=[pltpu.VMEM((8, 2 * D_FF), F32),
                        pltpu.VMEM((tm, D_FF), BF16)],
        compiler_params=pltpu.CompilerParams(dimension_semantics=("parallel", "arbitrary"),
                                             vmem_limit_bytes=VMEM_LIMIT),
        name="ffn",
    )(merged, h, wout, pmn, pfn, up, cw, cb, down, pon)


def _row_tile(lp):
    for tm in (640, 512, 384, 256, 128):
        if lp % tm == 0:
            return tm
    raise ValueError(f"padded length {lp} is not a multiple of {CH}")


def kernel(x, meta_tokens, pre_mix_norm, w_in, gla_gate_w2, gla_gate_b, ret_norm_w, gla_norm_w, w_out,
           post_mix_norm, pre_ffn_norm, ffn_up, ffn_conv_w, ffn_conv_b, ffn_down, post_ffn_norm):
    bsz, seq, d = x.shape
    depth = w_in.shape[0]
    assert d == D_MODEL and seq % CH == 0
    lp = CH + seq
    tm = _row_tile(lp)

    meta = jnp.broadcast_to(meta_tokens.astype(x.dtype)[None], (bsz, N_META, d))
    h = jnp.concatenate([jnp.zeros((bsz, PAD, d), x.dtype), meta, x], axis=1)

    half = RET_DK // 2
    inv = ROPE_BASE ** (-jnp.arange(half, dtype=F32) / half)
    pos = jnp.arange(lp, dtype=F32) - float(PAD)
    ang = pos[:, None] * inv[None, :]
    cos2 = jnp.concatenate([jnp.cos(ang), jnp.cos(ang)], axis=-1)
    sin2 = jnp.concatenate([-jnp.sin(ang), jnp.sin(ang)], axis=-1)

    vec = lambda v: v.reshape(1, -1).astype(F32)
    for l in range(depth):
        w_in_p = jnp.pad(w_in[l].astype(BF16), ((0, 0), (0, IN_PAD - IN_WIDTH)))
        w2p = jnp.pad(gla_gate_w2[l].astype(F32), ((0, LANES - GLA_GATE_RANK), (0, 0)))
        proj = _proj(h, vec(pre_mix_norm[l]), w_in_p, tm)
        merged = _mixer(proj, cos2, sin2, w2p, vec(gla_gate_b[l]), vec(ret_norm_w[l]), vec(gla_norm_w[l]))
        h = _ffn(merged, h, w_out[l].astype(BF16), vec(post_mix_norm[l]), vec(pre_ffn_norm[l]),
                 ffn_up[l].astype(BF16), ffn_conv_w[l].astype(F32), vec(ffn_conv_b[l]),
                 ffn_down[l].astype(BF16), vec(post_ffn_norm[l]), tm)
    return h[:, CH:]
```

```python
import functools

import numpy as np
import jax
import jax.numpy as jnp
from jax import lax
from jax.experimental import pallas as pl
from jax.experimental.pallas import tpu as pltpu

D_MODEL = 1024
N_META = 16
RET_HEADS = 4
RET_DK = 128
RET_DV = 128
GLA_HEADS = 4
GLA_DK = 64
GLA_DV = 128
GLA_GATE_RANK = 16
GLA_TAU = 16.0
D_FF = 2816
CONV_W = 3
ROPE_BASE = 10000.0
EPS = 1e-6

RET_QK = RET_HEADS * RET_DK
RET_V = RET_HEADS * RET_DV
GLA_QK = GLA_HEADS * GLA_DK
GLA_V = GLA_HEADS * GLA_DV
D_MIX = RET_V + GLA_V
IN_WIDTH = 2 * RET_QK + 2 * RET_V + 2 * GLA_QK + 2 * GLA_V + GLA_GATE_RANK

LANES = 128
SUBLANES = 8
CH = 128
PAD = CH - N_META
IN_PAD = 3712
GA_COL = IN_WIDTH - GLA_GATE_RANK
FF_TILE = 256
VMEM_LIMIT = 56 * 1024 * 1024
LOG2E = float(np.log2(np.e))

BF16 = jnp.bfloat16
F32 = jnp.float32

_NT = (((1,), (1,)), ((), ()))
_TN = (((0,), (0,)), ((), ()))


def _rms(x):
    return x * lax.rsqrt(jnp.mean(x * x, axis=-1, keepdims=True) + EPS)


def _silu(g):
    return g * (1.0 / (1.0 + jnp.exp(-g)))


def _const_spec(shape):
    nd = len(shape)
    return pl.BlockSpec(shape, lambda *_: (0,) * nd, pipeline_mode=pl.Buffered(1))


def _layer_spec(arr, l):
    nd = arr.ndim - 1
    return pl.BlockSpec((None,) + arr.shape[1:], lambda *_: (l,) + (0,) * nd, pipeline_mode=pl.Buffered(1))


def _proj_kernel(h_ref, nw_ref, w_ref, out_ref):
    hn = (_rms(h_ref[0]) * nw_ref[...]).astype(BF16)
    for c0 in range(0, IN_PAD, 512):
        cw = min(512, IN_PAD - c0)
        out_ref[0, :, c0:c0 + cw] = jnp.dot(hn, w_ref[:, c0:c0 + cw], preferred_element_type=F32)


def _proj_first_kernel(tm, x_ref, meta_ref, nw_ref, w_ref, out_ref, h_ref):
    j = pl.program_id(1)

    @pl.when(j == 0)
    def _():
        h_ref[0, :PAD] = jnp.zeros((PAD, D_MODEL), F32)
        h_ref[0, PAD:CH] = meta_ref[...]
        h_ref[0, CH:] = x_ref[:tm - CH]

    @pl.when(j > 0)
    def _():
        h_ref[0] = x_ref[...]

    _proj_kernel(h_ref, nw_ref, w_ref, out_ref)


def _x_window(tm, d):
    return pl.BlockSpec((None, pl.Element(tm), pl.Element(d)),
                        lambda i, j: (i, pl.multiple_of(jnp.maximum(j * tm - CH, 0), CH), 0))


def _proj_first(x, meta, nw, w, tm):
    b, seq, d = x.shape
    lp = CH + seq
    return pl.pallas_call(
        functools.partial(_proj_first_kernel, tm),
        grid=(b, lp // tm),
        in_specs=[_x_window(tm, d), _const_spec(meta.shape), _layer_spec(nw, 0), _layer_spec(w, 0)],
        out_specs=[pl.BlockSpec((1, tm, IN_PAD), lambda i, j: (i, j, 0)),
                   pl.BlockSpec((1, tm, d), lambda i, j: (i, j, 0))],
        out_shape=[jax.ShapeDtypeStruct((b, lp, IN_PAD), F32),
                   jax.ShapeDtypeStruct((b, lp, d), F32)],
        compiler_params=pltpu.CompilerParams(dimension_semantics=("parallel", "parallel"),
                                             vmem_limit_bytes=VMEM_LIMIT),
        name="proj_first",
    )(x, meta, nw, w)


def _proj(h, nw, w, l, tm):
    b, lp, d = h.shape
    return pl.pallas_call(
        _proj_kernel,
        grid=(b, lp // tm),
        in_specs=[pl.BlockSpec((1, tm, d), lambda i, j: (i, j, 0)),
                  _layer_spec(nw, l),
                  _layer_spec(w, l)],
        out_specs=pl.BlockSpec((1, tm, IN_PAD), lambda i, j: (i, j, 0)),
        out_shape=jax.ShapeDtypeStruct((b, lp, IN_PAD), F32),
        compiler_params=pltpu.CompilerParams(dimension_semantics=("parallel", "parallel"),
                                             vmem_limit_bytes=VMEM_LIMIT),
        name="proj",
    )(h, nw, w)


def _level_halves():
    out, m = [], CH // 2
    while m >= 1:
        out.append(m)
        m //= 2
    return tuple(out)


def _mixer_constants():
    c = CH
    idx = np.arange(c, dtype=np.float64)
    gam = 1.0 - 2.0 ** (-5.0 - np.arange(RET_HEADS, dtype=np.float64))
    kscale = RET_DK ** -0.5
    diff = idx[:, None] - idx[None, :]
    dmat = np.where(diff >= 0, gam[:, None, None] ** np.maximum(diff, 0.0), 0.0) * kscale
    zeta = np.repeat((gam[:, None] ** (c - 1.0 - idx)[None, :]).T, RET_DK, axis=1) * kscale
    xi = np.repeat((gam[:, None] ** (idx + 1.0)[None, :]).T, RET_DK, axis=1)
    gch = tuple(float(g ** c) for g in gam)
    s = np.arange(c)
    ml = [np.eye(c)]
    for m in _level_halves():
        blk = s // (2 * m)
        upper = (s % (2 * m)) >= m
        ml.append(((blk[:, None] == blk[None, :]) & upper[:, None] & (~upper)[None, :]).astype(np.float64))
    mlev = np.stack(ml, axis=0)
    tril = np.tril(np.ones((c, c)))
    return (jnp.asarray(dmat, F32), jnp.asarray(zeta, F32), jnp.asarray(xi, F32), gch,
            jnp.asarray(tril, BF16), jnp.asarray(mlev, F32))


def _block_ref(cum, m):
    parts = []
    for r0 in range(0, CH, 2 * m):
        parts.append(jnp.broadcast_to(cum[r0 + m - 1:r0 + m, :], (2 * m, cum.shape[1])))
    return parts[0] if len(parts) == 1 else jnp.concatenate(parts, axis=0)


def _mixer_kernel(gch, nb,
                  rq_ref, rk_ref, rv_ref, rg_ref, gq_ref, gk_ref, gv_ref, gr_ref, ga_ref,
                  cos_ref, sin_ref, w2_ref, gb_ref, rnw_ref, gnw_ref,
                  dmat_ref, zeta_ref, xi_ref, tril_ref, mlev_ref,
                  out_ref, sret_ref, sgla_ref):
    j = pl.program_id(0)

    @pl.when(j == 0)
    def _():
        sret_ref[...] = jnp.zeros_like(sret_ref)
        sgla_ref[...] = jnp.zeros_like(sgla_ref)

    cos = cos_ref[...]
    sin = sin_ref[...]
    row = lax.broadcasted_iota(jnp.int32, (CH, 1), 0)
    lane = lax.broadcasted_iota(jnp.int32, (1, GLA_QK), 1)
    even = (lane % LANES) < GLA_DK
    lane_ga = lax.broadcasted_iota(jnp.int32, (1, LANES), 1)
    lo_part = (lane_ga >= GLA_GATE_RANK) & (lane_ga < 2 * GLA_GATE_RANK)
    tril = tril_ref[...]

    for b in range(nb):
        for h in range(RET_HEADS):
            sl = slice(h * RET_DK, (h + 1) * RET_DK)
            q = rq_ref[b, :, sl]
            k = rk_ref[b, :, sl]
            q = q * cos + pltpu.roll(q, RET_DK // 2, 1) * sin
            k = k * cos + pltpu.roll(k, RET_DK // 2, 1) * sin
            vb = rv_ref[b, :, sl].astype(BF16)
            sc = lax.dot_general(q.astype(BF16), k.astype(BF16), _NT, preferred_element_type=F32) * dmat_ref[h]
            st = sret_ref[b, h]
            o = (jnp.dot(sc.astype(BF16), vb, preferred_element_type=F32)
                 + jnp.dot((q * xi_ref[:, sl]).astype(BF16), st.astype(BF16), preferred_element_type=F32))
            kz = (k * zeta_ref[:, sl]).astype(BF16)
            sret_ref[b, h] = st * gch[h] + lax.dot_general(kz, vb, _TN, preferred_element_type=F32)
            d = o - jnp.mean(o, axis=-1, keepdims=True)
            y = d * lax.rsqrt(jnp.mean(d * d, axis=-1, keepdims=True) + EPS) * rnw_ref[:, sl]
            out_ref[b, :, sl] = (y * _silu(rg_ref[b, :, sl])).astype(out_ref.dtype)

        g = ga_ref[b]
        g_hi = g.astype(BF16)
        g_lo = (g - g_hi.astype(F32)).astype(BF16)
        z = jnp.dot(jnp.where(lo_part, g_lo, g_hi), w2_ref[...], preferred_element_type=F32) + gb_ref[...]
        la = (jnp.minimum(z, 0.0) - jnp.log(1.0 + jnp.exp(-jnp.abs(z)))) * (LOG2E / GLA_TAU)
        la = jnp.where(j * CH + row >= PAD, la, 0.0)
        la_hi = la.astype(BF16)
        r1 = la - la_hi.astype(F32)
        la_mid = r1.astype(BF16)
        la_lo = (r1 - la_mid.astype(F32)).astype(BF16)
        cum = (jnp.dot(tril, la_hi, preferred_element_type=F32)
               + jnp.dot(tril, la_mid, preferred_element_type=F32)
               + jnp.dot(tril, la_lo, preferred_element_type=F32))
        last = cum[CH - 1:CH, :]

        qg = gq_ref[b] * (GLA_DK ** -0.5)
        kg = gk_ref[b]
        qg_e = jnp.where(even, qg, 0.0)
        qg_o = jnp.where(even, 0.0, qg)
        la_up = pltpu.roll(la, CH - 1, 0)
        la_dn = pltpu.roll(la, 1, 0)

        amat = [None] * GLA_HEADS
        for lv, m in enumerate((0,) + _level_halves()):
            if m == 0:
                qe, qo, kh = qg_e.astype(BF16), qg_o.astype(BF16), kg.astype(BF16)
            else:
                if m >= SUBLANES // 2:
                    ex = -jnp.abs(cum - _block_ref(cum, m))
                elif m == 2:
                    pos = row % 4
                    ex = jnp.where(pos == 0, la_up, jnp.where(pos == 2, la, jnp.where(pos == 3, la + la_dn, 0.0)))
                else:
                    ex = jnp.where(row % 2 == 1, la, 0.0)
                e = jnp.exp2(ex)
                qe, qo, kh = (qg_e * e).astype(BF16), (qg_o * e).astype(BF16), (kg * e).astype(BF16)
            msk = mlev_ref[lv]
            for p in range(GLA_HEADS // 2):
                pls = slice(p * LANES, (p + 1) * LANES)
                lhs = jnp.concatenate([qe[:, pls], qo[:, pls]], axis=0)
                sc = lax.dot_general(lhs, kh[:, pls], _NT, preferred_element_type=F32)
                for t in range(2):
                    term = sc[t * CH:(t + 1) * CH] * msk
                    amat[2 * p + t] = term if lv == 0 else amat[2 * p + t] + term

        ecum = jnp.exp2(cum)
        qd_e = (qg_e * ecum).astype(BF16)
        qd_o = (qg_o * ecum).astype(BF16)
        kd = kg * jnp.exp2(last - cum)
        kd_e = jnp.where(even, kd, 0.0).astype(BF16)
        kd_o = jnp.where(even, 0.0, kd).astype(BF16)
        dl = jnp.exp2(last)
        for p in range(GLA_HEADS // 2):
            pls = slice(p * LANES, (p + 1) * LANES)
            st = sgla_ref[b, :, pls]
            stb = st.astype(BF16)
            vbs = []
            for t in range(2):
                h = 2 * p + t
                sl = slice(h * GLA_DV, (h + 1) * GLA_DV)
                vb = gv_ref[b, :, sl].astype(BF16)
                vbs.append(vb)
                qd = (qd_e, qd_o)[t][:, pls]
                o = (jnp.dot(amat[h].astype(BF16), vb, preferred_element_type=F32)
                     + lax.dot_general(qd, stb, _NT, preferred_element_type=F32))
                y = o * lax.rsqrt(jnp.mean(o * o, axis=-1, keepdims=True) + EPS) * gnw_ref[:, sl]
                out_ref[b, :, RET_V + h * GLA_DV:RET_V + (h + 1) * GLA_DV] = (
                    y * _silu(gr_ref[b, :, sl])).astype(out_ref.dtype)
            kvt = lax.dot_general(jnp.concatenate(vbs, axis=0),
                                  jnp.concatenate([kd_e[:, pls], kd_o[:, pls]], axis=0),
                                  _TN, preferred_element_type=F32)
            sgla_ref[b, :, pls] = st * dl[:, pls] + kvt


def _mixer(proj, cos2, sin2, w2cat, gb, rnw, gnw, l):
    b, lp, _ = proj.shape
    dmat, zeta, xi, gch, tril, mlev = _mixer_constants()

    def col(width, blk):
        return pl.BlockSpec((b, CH, width), lambda j: (0, j, blk))

    in_specs = [
        col(RET_QK, 0), col(RET_QK, 1), col(RET_V, 2), col(RET_V, 3),
        col(GLA_QK, 2048 // GLA_QK), col(GLA_QK, 2304 // GLA_QK),
        col(GLA_V, 2560 // GLA_V), col(GLA_V, 3072 // GLA_V),
        col(LANES, GA_COL // LANES),
        pl.BlockSpec((CH, RET_DK), lambda j: (j, 0)),
        pl.BlockSpec((CH, RET_DK), lambda j: (j, 0)),
        _layer_spec(w2cat, l), _layer_spec(gb, l), _layer_spec(rnw, l), _layer_spec(gnw, l),
        _const_spec(dmat.shape), _const_spec(zeta.shape), _const_spec(xi.shape),
        _const_spec(tril.shape), _const_spec(mlev.shape),
    ]
    return pl.pallas_call(
        functools.partial(_mixer_kernel, gch, b),
        grid=(lp // CH,),
        in_specs=in_specs,
        out_specs=pl.BlockSpec((b, CH, D_MIX), lambda j: (0, j, 0)),
        out_shape=jax.ShapeDtypeStruct((b, lp, D_MIX), BF16),
        scratch_shapes=[pltpu.VMEM((b, RET_HEADS, RET_DK, RET_DV), F32),
                        pltpu.VMEM((b, GLA_DV, GLA_QK), F32)],
        compiler_params=pltpu.CompilerParams(dimension_semantics=("arbitrary",),
                                             vmem_limit_bytes=VMEM_LIMIT),
        name="mixer",
    )(*([proj] * 9), cos2, sin2, w2cat, gb, rnw, gnw, dmat, zeta, xi, tril, mlev)


def _gelu_tanh(x):
    return 0.5 * x * (1.0 + jnp.tanh(np.sqrt(2.0 / np.pi) * (x + 0.044715 * (x * x * x))))


def _ffn_tile(tm, m_ref, h_ref, wout_ref, pmn_ref, pfn_ref, up_ref, cw_ref, cb_ref, down_ref, pon_ref,
              halo_ref, act_ref):
    i = pl.program_id(1)

    @pl.when(i == 0)
    def _():
        halo_ref[...] = jnp.zeros_like(halo_ref)

    m = jnp.dot(m_ref[0], wout_ref[...], preferred_element_type=F32)
    h1 = h_ref[0] + _rms(m) * pmn_ref[...]
    hn = (_rms(h1) * pfn_ref[...]).astype(BF16)
    sub = lax.broadcasted_iota(jnp.int32, (8, 1), 0)

    def conv(c0):
        cs = slice(c0, c0 + FF_TILE)
        u = jnp.dot(hn, up_ref[:, cs], preferred_element_type=F32)
        prev = halo_ref[:, cs]
        halo_ref[:, cs] = u[tm - 8:tm]
        s1 = pltpu.roll(u, 1, 0)
        s2 = pltpu.roll(u, 2, 0)
        top1 = jnp.where(sub < 1, pltpu.roll(prev, 1, 0), s1[:8])
        top2 = jnp.where(sub < 2, pltpu.roll(prev, 2, 0), s2[:8])
        s1 = jnp.concatenate([top1, s1[8:]], axis=0)
        s2 = jnp.concatenate([top2, s2[8:]], axis=0)
        return (cb_ref[:, cs] + u * cw_ref[2:3, cs] + s1 * cw_ref[1:2, cs] + s2 * cw_ref[0:1, cs])

    for c0 in range(0, D_FF, FF_TILE):
        a = conv(c0)
        g = conv(D_FF + c0)
        act_ref[:, c0:c0 + FF_TILE] = (_gelu_tanh(a) * g).astype(BF16)

    f = jnp.dot(act_ref[...], down_ref[...], preferred_element_type=F32)
    h2 = h1 + _rms(f) * pon_ref[...]
    row = i * tm + lax.broadcasted_iota(jnp.int32, (tm, 1), 0)
    return jnp.where(row >= PAD, h2, 0.0)


def _ffn_kernel(tm, *refs):
    out_ref, halo_ref, act_ref = refs[-3:]
    out_ref[0] = _ffn_tile(tm, *refs[:-3], halo_ref, act_ref)


def _ffn_last_kernel(tm, *refs):
    out_ref, halo_ref, act_ref, obuf_ref, osem = refs[-5:]
    b, i = pl.program_id(0), pl.program_id(1)
    nt = pl.num_programs(1)
    step = b * nt + i
    slot = step % 2

    def first_copy(s, bb):
        return pltpu.make_async_copy(obuf_ref.at[s, pl.ds(CH, tm - CH)],
                                     out_ref.at[bb, pl.ds(0, tm - CH)], osem.at[s])

    def tile_copy(s, bb, ii):
        return pltpu.make_async_copy(obuf_ref.at[s], out_ref.at[bb, pl.ds(ii * tm - CH, tm)], osem.at[s])

    def wait_step(s, bb, ii):
        @pl.when(ii == 0)
        def _():
            first_copy(s, bb).wait()

        @pl.when(ii > 0)
        def _():
            tile_copy(s, bb, ii).wait()

    res = _ffn_tile(tm, *refs[:-5], halo_ref, act_ref)

    @pl.when(step >= 2)
    def _():
        prev = step - 2
        wait_step(slot, prev // nt, prev % nt)

    obuf_ref[slot] = res

    @pl.when(i == 0)
    def _():
        first_copy(slot, b).start()

    @pl.when(i > 0)
    def _():
        tile_copy(slot, b, i).start()

    @pl.when(step == pl.num_programs(0) * nt - 1)
    def _():
        wait_step(1 - slot, (step - 1) // nt, (step - 1) % nt)
        wait_step(slot, b, i)


def _ffn(merged, h, wout, pmn, pfn, up, cw, cb, down, pon, l, tm, last):
    b, lp, d = h.shape
    assert lp // tm >= 2
    row = lambda width: pl.BlockSpec((1, tm, width), lambda i, j: (i, j, 0))
    scratch = [pltpu.VMEM((8, 2 * D_FF), F32), pltpu.VMEM((tm, D_FF), BF16)]
    if last:
        scratch += [pltpu.VMEM((2, tm, d), F32), pltpu.SemaphoreType.DMA((2,))]
    return pl.pallas_call(
        functools.partial(_ffn_last_kernel if last else _ffn_kernel, tm),
        grid=(b, lp // tm),
        in_specs=[row(D_MIX), row(d)] + [_layer_spec(a, l) for a in (wout, pmn, pfn, up, cw, cb, down, pon)],
        out_specs=pl.BlockSpec(memory_space=pl.ANY) if last else row(d),
        out_shape=jax.ShapeDtypeStruct((b, lp - CH, d) if last else (b, lp, d), F32),
        scratch_shapes=scratch,
        compiler_params=pltpu.CompilerParams(
            dimension_semantics=("arbitrary", "arbitrary") if last else ("parallel", "arbitrary"),
            vmem_limit_bytes=VMEM_LIMIT),
        name="ffn_last" if last else "ffn",
    )(merged, h, wout, pmn, pfn, up, cw, cb, down, pon)


def _rope_tables(lp):
    half = RET_DK // 2
    inv = ROPE_BASE ** (-np.arange(half, dtype=np.float64) / half)
    ang = (np.arange(lp, dtype=np.float64) - PAD)[:, None] * inv[None, :]
    c, s = np.cos(ang), np.sin(ang)
    return (jnp.asarray(np.concatenate([c, c], axis=-1), F32),
            jnp.asarray(np.concatenate([-s, s], axis=-1), F32))


def _row_tile(lp):
    for tm in (640, 512, 384, 256, 128):
        if lp % tm == 0:
            return tm
    raise ValueError(f"padded length {lp} is not a multiple of {CH}")


def kernel(x, meta_tokens, pre_mix_norm, w_in, gla_gate_w2, gla_gate_b, ret_norm_w, gla_norm_w, w_out,
           post_mix_norm, pre_ffn_norm, ffn_up, ffn_conv_w, ffn_conv_b, ffn_down, post_ffn_norm):
    bsz, seq, d = x.shape
    depth = w_in.shape[0]
    assert d == D_MODEL and seq % CH == 0
    lp = CH + seq
    tm = _row_tile(lp)

    cos2, sin2 = _rope_tables(lp)

    vec = lambda v: v.reshape(depth, 1, -1).astype(F32)
    w_b = w_in.astype(BF16)
    w_ga = w_b[:, :, GA_COL:]
    w_in_p = jnp.concatenate(
        [w_b, w_ga, w_ga, jnp.zeros((depth, d, IN_PAD - IN_WIDTH - 2 * GLA_GATE_RANK), BF16)], axis=2)
    w2 = gla_gate_w2.astype(F32)
    w2_hi = w2.astype(BF16)
    w2_lo = (w2 - w2_hi.astype(F32)).astype(BF16)
    w2cat = jnp.concatenate(
        [w2_hi, w2_hi, w2_lo, jnp.zeros((depth, LANES - 3 * GLA_GATE_RANK, GLA_QK), BF16)], axis=1)
    pre_mix, gate_b, ret_nw, gla_nw = vec(pre_mix_norm), vec(gla_gate_b), vec(ret_norm_w), vec(gla_norm_w)
    ffn_params = (w_out.astype(BF16), vec(post_mix_norm), vec(pre_ffn_norm), ffn_up.astype(BF16),
                  ffn_conv_w.astype(F32), vec(ffn_conv_b), ffn_down.astype(BF16), vec(post_ffn_norm))
    h = None
    for l in range(depth):
        if l == 0:
            proj, h = _proj_first(x.astype(F32), meta_tokens.astype(F32), pre_mix, w_in_p, tm)
        else:
            proj = _proj(h, pre_mix, w_in_p, l, tm)
        merged = _mixer(proj, cos2, sin2, w2cat, gate_b, ret_nw, gla_nw, l)
        h = _ffn(merged, h, *ffn_params, l, tm, last=(l == depth - 1))
    return h
```

```python
import functools

import numpy as np
import jax
import jax.numpy as jnp
from jax import lax
from jax.experimental import pallas as pl
from jax.experimental.pallas import tpu as pltpu

D_MODEL = 1024
N_META = 16
RET_HEADS = 4
RET_DK = 128
RET_DV = 128
GLA_HEADS = 4
GLA_DK = 64
GLA_DV = 128
GLA_GATE_RANK = 16
GLA_TAU = 16.0
D_FF = 2816
CONV_W = 3
ROPE_BASE = 10000.0
EPS = 1e-6

RET_QK = RET_HEADS * RET_DK
RET_V = RET_HEADS * RET_DV
GLA_QK = GLA_HEADS * GLA_DK
GLA_V = GLA_HEADS * GLA_DV
D_MIX = RET_V + GLA_V
IN_WIDTH = 2 * RET_QK + 2 * RET_V + 2 * GLA_QK + 2 * GLA_V + GLA_GATE_RANK

LANES = 128
SUBLANES = 8
CH = 128
PAD = CH - N_META
C_RQ, C_RK, C_RV, C_RG = 0, RET_QK, 2 * RET_QK, 2 * RET_QK + RET_V
C_GQ = C_RG + RET_V
C_GK, C_GV = C_GQ + GLA_QK, C_GQ + 2 * GLA_QK
C_GR = C_GV + GLA_V
MAIN_W = C_GR + GLA_V
FF_TILE = 256
VMEM_LIMIT = 56 * 1024 * 1024
LOG2E = float(np.log2(np.e))

BF16 = jnp.bfloat16
F32 = jnp.float32

_NT = (((1,), (1,)), ((), ()))
_TN = (((0,), (0,)), ((), ()))


def _rms(x):
    return x * lax.rsqrt(jnp.mean(x * x, axis=-1, keepdims=True) + EPS)


def _silu(g):
    return g * (1.0 / (1.0 + jnp.exp(-g)))


def _const_spec(shape):
    nd = len(shape)
    return pl.BlockSpec(shape, lambda *_: (0,) * nd, pipeline_mode=pl.Buffered(1))


def _layer_spec(arr, l):
    nd = arr.ndim - 1
    return pl.BlockSpec((None,) + arr.shape[1:], lambda *_: (l,) + (0,) * nd, pipeline_mode=pl.Buffered(1))


def _proj_tile(tm, h_ref, nw_ref, w_ref, wga_ref, w2_ref, gb_ref, cos_ref, sin_ref, zeta_ref,
               wo_f32_ref, up_f32_ref, dn_f32_ref,
               ret_ref, gate_ref, gqk_ref, gv_ref, la_ref, wo_b16_ref, up_b16_ref, dn_b16_ref):
    j = pl.program_id(1)
    hn = (_rms(h_ref[0]) * nw_ref[...]).astype(BF16)

    def cols(c0, width):
        return jnp.dot(hn, w_ref[:, c0:c0 + width], preferred_element_type=F32)

    cos = cos_ref[...]
    sin = sin_ref[...]

    def rope(t, h):
        x = t[:, h * RET_DK:(h + 1) * RET_DK]
        return x * cos + pltpu.roll(x, RET_DK // 2, 1) * sin

    def put_q(q):
        for h in range(RET_HEADS):
            ret_ref[0, :, h * RET_DK:(h + 1) * RET_DK] = rope(q, h).astype(BF16)

    def put_k(k):
        for h in range(RET_HEADS):
            sl = slice(h * RET_DK, (h + 1) * RET_DK)
            kr = rope(k, h)
            ret_ref[0, :, RET_QK + h * RET_DK:RET_QK + (h + 1) * RET_DK] = kr.astype(BF16)
            ret_ref[0, :, 2 * RET_QK + h * RET_DK:2 * RET_QK + (h + 1) * RET_DK] = (
                kr * zeta_ref[:, sl]).astype(BF16)

    def put_v(v):
        ret_ref[0, :, 3 * RET_QK:] = v.astype(BF16)

    def put_rg(g):
        gate_ref[0, :, :RET_V] = _silu(g)

    def put_gr(g):
        gate_ref[0, :, RET_V:] = _silu(g)

    def put_gqk(t):
        gqk_ref[0, :, :GLA_QK] = t[:, :GLA_QK] * (GLA_DK ** -0.5)
        gqk_ref[0, :, GLA_QK:] = t[:, GLA_QK:]

    def put_gv(v):
        gv_ref[0] = v.astype(BF16)

    def put_la(g):
        g_hi = g.astype(BF16)
        g_lo = (g - g_hi.astype(F32)).astype(BF16)
        lane = lax.broadcasted_iota(jnp.int32, (1, LANES), 1)
        lo_part = (lane >= GLA_GATE_RANK) & (lane < 2 * GLA_GATE_RANK)
        z = jnp.dot(jnp.where(lo_part, g_lo, g_hi), w2_ref[...], preferred_element_type=F32) + gb_ref[...]
        la = (jnp.minimum(z, 0.0) - jnp.log(1.0 + jnp.exp(-jnp.abs(z)))) * (LOG2E / GLA_TAU)
        row = j * tm + lax.broadcasted_iota(jnp.int32, (tm, 1), 0)
        la_ref[0] = jnp.where(row >= PAD, la, 0.0)

    def cast_weights(_):
        for src, dst in ((wo_f32_ref, wo_b16_ref), (up_f32_ref, up_b16_ref), (dn_f32_ref, dn_b16_ref)):
            dst[...] = src[...].astype(BF16)

    stages = [
        (lambda: jnp.dot(hn, wga_ref[...], preferred_element_type=F32), put_la),
        (lambda: cols(C_RQ, RET_QK), put_q),
        (lambda: cols(C_RK, RET_QK), put_k),
        (lambda: cols(C_RG, RET_V), put_rg),
        (lambda: cols(C_GR, GLA_V), put_gr),
        (lambda: cols(C_GQ, 2 * GLA_QK), put_gqk),
        (lambda: cols(C_RV, RET_V), put_v),
        (lambda: cols(C_GV, GLA_V), put_gv),
        (lambda: None, cast_weights),
    ]
    pending = None
    for matmul, epilogue in stages:
        res = matmul()
        if pending is not None:
            pending[1](pending[0])
        pending = (res, epilogue)
    pending[1](pending[0])


def _proj_kernel(tm, *refs):
    _proj_tile(tm, *refs)


def _proj_first_kernel(tm, x_ref, meta_ref, *refs):
    h_ref = refs[-1]
    j = pl.program_id(1)

    @pl.when(j == 0)
    def _():
        h_ref[0, :PAD] = jnp.zeros((PAD, D_MODEL), F32)
        h_ref[0, PAD:CH] = meta_ref[...]
        h_ref[0, CH:] = x_ref[:tm - CH]

    @pl.when(j > 0)
    def _():
        h_ref[0] = x_ref[...]

    _proj_tile(tm, h_ref, *refs[:-1])


def _x_window(tm, d):
    return pl.BlockSpec((None, pl.Element(tm), pl.Element(d)),
                        lambda i, j: (i, pl.multiple_of(jnp.maximum(j * tm - CH, 0), CH), 0))


def _slab_specs(arr, l, axis, unit, steps, nt):
    n_units = arr.shape[axis] // unit
    n_slabs = max(s for s in range(1, n_units + 1) if n_units % s == 0 and s <= steps)
    size = arr.shape[axis] // n_slabs
    shape = tuple(size if a == axis else arr.shape[a] for a in (1, 2))

    def pos(i, j):
        s = jnp.minimum(i * nt + j, n_slabs - 1)
        return (s, 0) if axis == 1 else (0, s)

    return (pl.BlockSpec((None,) + shape, lambda i, j: (l,) + pos(i, j)),
            pl.BlockSpec(shape, pos),
            jax.ShapeDtypeStruct(arr.shape[1:], BF16))


def _proj(h_or_x, meta, nw, w, wga, w2cat, gb, cos2, sin2, zeta_t, ffn_w, l, tm):
    first = meta is not None
    b, d = h_or_x.shape[0], h_or_x.shape[2]
    lp = h_or_x.shape[1] + (CH if first else 0)
    nt = lp // tm
    row = lambda width: pl.BlockSpec((1, tm, width), lambda i, j: (i, j, 0))
    tab = pl.BlockSpec((tm, RET_DK), lambda i, j: (j, 0))
    w_main = pl.BlockSpec((None, d, MAIN_W), lambda i, j: (l, 0, 0), pipeline_mode=pl.Buffered(1))
    slabs = [_slab_specs(ffn_w[0], l, 1, 4 * SUBLANES, b * nt, nt),
             _slab_specs(ffn_w[1], l, 2, 2 * LANES, b * nt, nt),
             _slab_specs(ffn_w[2], l, 1, 4 * SUBLANES, b * nt, nt)]
    lead = [_x_window(tm, d), _const_spec(meta.shape)] if first else [row(d)]
    in_specs = lead + [_layer_spec(nw, l), w_main, _layer_spec(wga, l), _layer_spec(w2cat, l), _layer_spec(gb, l),
                       tab, tab, _const_spec(zeta_t.shape)] + [s[0] for s in slabs]
    out_specs = ([row(4 * RET_QK), row(RET_V + GLA_V), row(2 * GLA_QK), row(GLA_V), row(GLA_QK)]
                 + [s[1] for s in slabs])
    out_shape = [jax.ShapeDtypeStruct((b, lp, 4 * RET_QK), BF16),
                 jax.ShapeDtypeStruct((b, lp, RET_V + GLA_V), F32),
                 jax.ShapeDtypeStruct((b, lp, 2 * GLA_QK), F32),
                 jax.ShapeDtypeStruct((b, lp, GLA_V), BF16),
                 jax.ShapeDtypeStruct((b, lp, GLA_QK), F32)] + [s[2] for s in slabs]
    if first:
        out_specs.append(row(d))
        out_shape.append(jax.ShapeDtypeStruct((b, lp, d), F32))
    args = ([h_or_x, meta] if first else [h_or_x]) + [nw, w, wga, w2cat, gb, cos2, sin2, zeta_t, *ffn_w]
    return pl.pallas_call(
        functools.partial(_proj_first_kernel if first else _proj_kernel, tm),
        grid=(b, nt),
        in_specs=in_specs,
        out_specs=out_specs,
        out_shape=out_shape,
        compiler_params=pltpu.CompilerParams(dimension_semantics=("arbitrary", "arbitrary"),
                                             vmem_limit_bytes=VMEM_LIMIT),
        name="proj_first" if first else "proj",
    )(*args)


def _level_halves():
    out, m = [], CH // 2
    while m >= 1:
        out.append(m)
        m //= 2
    return tuple(out)


def _mixer_constants(tm):
    c = CH
    idx = np.arange(c, dtype=np.float64)
    gam = 1.0 - 2.0 ** (-5.0 - np.arange(RET_HEADS, dtype=np.float64))
    kscale = RET_DK ** -0.5
    diff = idx[:, None] - idx[None, :]
    dmat = np.where(diff >= 0, gam[:, None, None] ** np.maximum(diff, 0.0), 0.0) * kscale
    zeta = np.repeat((gam[:, None] ** (c - 1.0 - idx)[None, :]).T, RET_DK, axis=1) * kscale
    xi = np.repeat((gam[:, None] ** (idx + 1.0)[None, :]).T, RET_DK, axis=1)
    gch = tuple(float(g ** c) for g in gam)
    s = np.arange(c)
    ml = [np.eye(c)]
    for m in _level_halves():
        blk = s // (2 * m)
        upper = (s % (2 * m)) >= m
        ml.append(((blk[:, None] == blk[None, :]) & upper[:, None] & (~upper)[None, :]).astype(np.float64))
    mlev = np.stack(ml, axis=0)
    tril = np.tril(np.ones((c, c)))
    return dict(dmat=jnp.asarray(dmat, F32), zeta_t=jnp.asarray(np.tile(zeta, (tm // c, 1)), F32),
                xi=jnp.asarray(xi, F32), gch=gch, tril=jnp.asarray(tril, BF16), mlev=jnp.asarray(mlev, F32))


def _block_ref(cum, m):
    parts = []
    for r0 in range(0, CH, 2 * m):
        parts.append(jnp.broadcast_to(cum[r0 + m - 1:r0 + m, :], (2 * m, cum.shape[1])))
    return parts[0] if len(parts) == 1 else jnp.concatenate(parts, axis=0)


def _mixer_kernel(gch, nb,
                  ret_ref, gate_ref, gqk_ref, gv_ref, la_ref, rnw_ref, gnw_ref,
                  dmat_ref, xi_ref, tril_ref, mlev_ref,
                  out_ref, sret_ref, sgla_ref):
    j = pl.program_id(0)

    @pl.when(j == 0)
    def _():
        sret_ref[...] = jnp.zeros_like(sret_ref)
        sgla_ref[...] = jnp.zeros_like(sgla_ref)

    row = lax.broadcasted_iota(jnp.int32, (CH, 1), 0)
    lane = lax.broadcasted_iota(jnp.int32, (1, GLA_QK), 1)
    even = (lane % LANES) < GLA_DK
    tril = tril_ref[...]

    def retention(b, h):
        sl = slice(h * RET_DK, (h + 1) * RET_DK)
        q = ret_ref[b, :, sl]
        k = ret_ref[b, :, RET_QK + h * RET_DK:RET_QK + (h + 1) * RET_DK]
        kz = ret_ref[b, :, 2 * RET_QK + h * RET_DK:2 * RET_QK + (h + 1) * RET_DK]
        vb = ret_ref[b, :, 3 * RET_QK + h * RET_DV:3 * RET_QK + (h + 1) * RET_DV]
        sc = lax.dot_general(q, k, _NT, preferred_element_type=F32) * dmat_ref[h]
        st = sret_ref[b, h]
        o = (jnp.dot(sc.astype(BF16), vb, preferred_element_type=F32)
             + jnp.dot(q, st.astype(BF16), preferred_element_type=F32) * xi_ref[:, sl])
        sret_ref[b, h] = st * gch[h] + lax.dot_general(kz, vb, _TN, preferred_element_type=F32)
        d = o - jnp.mean(o, axis=-1, keepdims=True)
        y = d * lax.rsqrt(jnp.mean(d * d, axis=-1, keepdims=True) + EPS) * rnw_ref[:, sl]
        out_ref[b, :, sl] = (y * gate_ref[b, :, sl]).astype(out_ref.dtype)

    def gla_prepare(b):
        la = la_ref[b]
        la_hi = la.astype(BF16)
        r1 = la - la_hi.astype(F32)
        la_mid = r1.astype(BF16)
        la_lo = (r1 - la_mid.astype(F32)).astype(BF16)
        cum = (jnp.dot(tril, la_hi, preferred_element_type=F32)
               + jnp.dot(tril, la_mid, preferred_element_type=F32)
               + jnp.dot(tril, la_lo, preferred_element_type=F32))
        qg = gqk_ref[b, :, :GLA_QK]
        kg = gqk_ref[b, :, GLA_QK:]
        return dict(la=la, cum=cum, kg=kg, qg_e=jnp.where(even, qg, 0.0), qg_o=jnp.where(even, 0.0, qg),
                    la_up=pltpu.roll(la, CH - 1, 0), la_dn=pltpu.roll(la, 1, 0),
                    amat=[None] * GLA_HEADS)

    levels = (0,) + _level_halves()

    def gla_level(p, pr, lv):
        pls = slice(pr * LANES, (pr + 1) * LANES)
        la, cum, kg, qg_e, qg_o = (p[n][:, pls] for n in ("la", "cum", "kg", "qg_e", "qg_o"))
        m = levels[lv]
        if m == 0:
            qe, qo, kh = qg_e.astype(BF16), qg_o.astype(BF16), kg.astype(BF16)
        else:
            if m >= SUBLANES // 2:
                ex = -jnp.abs(cum - _block_ref(cum, m))
            elif m == 2:
                pos = row % 4
                ex = jnp.where(pos == 0, p["la_up"][:, pls],
                               jnp.where(pos == 2, la, jnp.where(pos == 3, la + p["la_dn"][:, pls], 0.0)))
            else:
                ex = jnp.where(row % 2 == 1, la, 0.0)
            e = jnp.exp2(ex)
            qe, qo, kh = (qg_e * e).astype(BF16), (qg_o * e).astype(BF16), (kg * e).astype(BF16)
        msk = mlev_ref[lv]
        sc = lax.dot_general(jnp.concatenate([qe, qo], axis=0), kh, _NT, preferred_element_type=F32)
        for t in range(2):
            term = sc[t * CH:(t + 1) * CH] * msk
            p["amat"][2 * pr + t] = term if lv == 0 else p["amat"][2 * pr + t] + term

    def gla_decays(p):
        cum, kg = p["cum"], p["kg"]
        last = cum[CH - 1:CH, :]
        ecum = jnp.exp2(cum)
        kd = kg * jnp.exp2(last - cum)
        p.update(qd=((p["qg_e"] * ecum).astype(BF16), (p["qg_o"] * ecum).astype(BF16)),
                 kd=(jnp.where(even, kd, 0.0).astype(BF16), jnp.where(even, 0.0, kd).astype(BF16)),
                 dl=jnp.exp2(last))

    def gla_finish(b, p, pr):
        pls = slice(pr * LANES, (pr + 1) * LANES)
        st = sgla_ref[b, :, pls]
        stb = st.astype(BF16)
        vbs = []
        for t in range(2):
            h = 2 * pr + t
            sl = slice(h * GLA_DV, (h + 1) * GLA_DV)
            osl = slice(RET_V + h * GLA_DV, RET_V + (h + 1) * GLA_DV)
            vb = gv_ref[b, :, sl]
            vbs.append(vb)
            o = (jnp.dot(p["amat"][h].astype(BF16), vb, preferred_element_type=F32)
                 + lax.dot_general(p["qd"][t][:, pls], stb, _NT, preferred_element_type=F32))
            y = o * lax.rsqrt(jnp.mean(o * o, axis=-1, keepdims=True) + EPS) * gnw_ref[:, sl]
            out_ref[b, :, osl] = (y * gate_ref[b, :, osl]).astype(out_ref.dtype)
        kvt = lax.dot_general(jnp.concatenate(vbs, axis=0),
                              jnp.concatenate([p["kd"][0][:, pls], p["kd"][1][:, pls]], axis=0),
                              _TN, preferred_element_type=F32)
        sgla_ref[b, :, pls] = st * p["dl"][:, pls] + kvt

    preps = [gla_prepare(b) for b in range(nb)]
    fillers = []
    for b in range(nb):
        fillers += [functools.partial(retention, b, h) for h in range(RET_HEADS)]
    for b in range(nb):
        for pr in range(GLA_HEADS // 2):
            for lv in range(len(levels)):
                gla_level(preps[b], pr, lv)
                if fillers and lv % 2 == 1:
                    fillers.pop(0)()
        gla_decays(preps[b])
        fillers += [functools.partial(gla_finish, b, preps[b], pr) for pr in range(GLA_HEADS // 2)]
    for f in fillers:
        f()


def _mixer(ret, gate, gqk, gv, la, rnw, gnw, consts, l):
    b, lp, _ = gate.shape

    def blk(arr):
        return pl.BlockSpec((b, CH, arr.shape[2]), lambda j: (0, j, 0))

    tabs = [consts[k] for k in ("dmat", "xi", "tril", "mlev")]
    return pl.pallas_call(
        functools.partial(_mixer_kernel, consts["gch"], b),
        grid=(lp // CH,),
        in_specs=[blk(ret), blk(gate), blk(gqk), blk(gv), blk(la), _layer_spec(rnw, l), _layer_spec(gnw, l)]
        + [_const_spec(t.shape) for t in tabs],
        out_specs=pl.BlockSpec((b, CH, D_MIX), lambda j: (0, j, 0)),
        out_shape=jax.ShapeDtypeStruct((b, lp, D_MIX), BF16),
        scratch_shapes=[pltpu.VMEM((b, RET_HEADS, RET_DK, RET_DV), F32),
                        pltpu.VMEM((b, GLA_DV, GLA_QK), F32)],
        compiler_params=pltpu.CompilerParams(dimension_semantics=("arbitrary",),
                                             vmem_limit_bytes=VMEM_LIMIT),
        name="mixer",
    )(ret, gate, gqk, gv, la, rnw, gnw, *tabs)


def _gelu_tanh(x):
    return 0.5 * x * (1.0 + jnp.tanh(np.sqrt(2.0 / np.pi) * (x + 0.044715 * (x * x * x))))


def _ffn_tile(tm, m_ref, h_ref, wout_ref, pmn_ref, pfn_ref, up_ref, cw_ref, cb_ref, down_ref, pon_ref,
              halo_ref, act_ref):
    i = pl.program_id(1)

    @pl.when(i == 0)
    def _():
        halo_ref[...] = jnp.zeros_like(halo_ref)

    m = jnp.dot(m_ref[0], wout_ref[...], preferred_element_type=F32)
    h1 = h_ref[0] + _rms(m) * pmn_ref[...]
    hn = (_rms(h1) * pfn_ref[...]).astype(BF16)
    sub = lax.broadcasted_iota(jnp.int32, (8, 1), 0)

    def conv(c0):
        cs = slice(c0, c0 + FF_TILE)
        u = jnp.dot(hn, up_ref[:, cs], preferred_element_type=F32)
        prev = halo_ref[:, cs]
        halo_ref[:, cs] = u[tm - 8:tm]
        s1 = pltpu.roll(u, 1, 0)
        s2 = pltpu.roll(u, 2, 0)
        top1 = jnp.where(sub < 1, pltpu.roll(prev, 1, 0), s1[:8])
        top2 = jnp.where(sub < 2, pltpu.roll(prev, 2, 0), s2[:8])
        s1 = jnp.concatenate([top1, s1[8:]], axis=0)
        s2 = jnp.concatenate([top2, s2[8:]], axis=0)
        return (cb_ref[:, cs] + u * cw_ref[2:3, cs] + s1 * cw_ref[1:2, cs] + s2 * cw_ref[0:1, cs])

    for c0 in range(0, D_FF, FF_TILE):
        a = conv(c0)
        g = conv(D_FF + c0)
        act_ref[:, c0:c0 + FF_TILE] = (_gelu_tanh(a) * g).astype(BF16)

    f = jnp.dot(act_ref[...], down_ref[...], preferred_element_type=F32)
    h2 = h1 + _rms(f) * pon_ref[...]
    row = i * tm + lax.broadcasted_iota(jnp.int32, (tm, 1), 0)
    return jnp.where(row >= PAD, h2, 0.0)


def _ffn_kernel(tm, *refs):
    out_ref, halo_ref, act_ref = refs[-3:]
    out_ref[0] = _ffn_tile(tm, *refs[:-3], halo_ref, act_ref)


def _ffn_last_kernel(tm, *refs):
    out_ref, halo_ref, act_ref, obuf_ref, osem = refs[-5:]
    b, i = pl.program_id(0), pl.program_id(1)
    nt = pl.num_programs(1)
    step = b * nt + i
    slot = step % 2

    def first_copy(s, bb):
        return pltpu.make_async_copy(obuf_ref.at[s, pl.ds(CH, tm - CH)],
                                     out_ref.at[bb, pl.ds(0, tm - CH)], osem.at[s])

    def tile_copy(s, bb, ii):
        return pltpu.make_async_copy(obuf_ref.at[s], out_ref.at[bb, pl.ds(ii * tm - CH, tm)], osem.at[s])

    def wait_step(s, bb, ii):
        @pl.when(ii == 0)
        def _():
            first_copy(s, bb).wait()

        @pl.when(ii > 0)
        def _():
            tile_copy(s, bb, ii).wait()

    res = _ffn_tile(tm, *refs[:-5], halo_ref, act_ref)

    @pl.when(step >= 2)
    def _():
        prev = step - 2
        wait_step(slot, prev // nt, prev % nt)

    obuf_ref[slot] = res

    @pl.when(i == 0)
    def _():
        first_copy(slot, b).start()

    @pl.when(i > 0)
    def _():
        tile_copy(slot, b, i).start()

    @pl.when(step == pl.num_programs(0) * nt - 1)
    def _():
        wait_step(1 - slot, (step - 1) // nt, (step - 1) % nt)
        wait_step(slot, b, i)


def _ffn(merged, h, wout, pmn, pfn, up, cw, cb, down, pon, l, tm, last):
    b, lp, d = h.shape
    assert lp // tm >= 2
    row = lambda width: pl.BlockSpec((1, tm, width), lambda i, j: (i, j, 0))
    scratch = [pltpu.VMEM((8, 2 * D_FF), F32), pltpu.VMEM((tm, D_FF), BF16)]
    if last:
        scratch += [pltpu.VMEM((2, tm, d), F32), pltpu.SemaphoreType.DMA((2,))]
    return pl.pallas_call(
        functools.partial(_ffn_last_kernel if last else _ffn_kernel, tm),
        grid=(b, lp // tm),
        in_specs=[row(D_MIX), row(d)] + [_const_spec(a.shape) if a.ndim == 2 else _layer_spec(a, l)
                                         for a in (wout, pmn, pfn, up, cw, cb, down, pon)],
        out_specs=pl.BlockSpec(memory_space=pl.ANY) if last else row(d),
        out_shape=jax.ShapeDtypeStruct((b, lp - CH, d) if last else (b, lp, d), F32),
        scratch_shapes=scratch,
        compiler_params=pltpu.CompilerParams(
            dimension_semantics=("arbitrary", "arbitrary") if last else ("parallel", "arbitrary"),
            vmem_limit_bytes=VMEM_LIMIT),
        name="ffn_last" if last else "ffn",
    )(merged, h, wout, pmn, pfn, up, cw, cb, down, pon)


def _rope_tables(lp):
    half = RET_DK // 2
    inv = ROPE_BASE ** (-np.arange(half, dtype=np.float64) / half)
    ang = (np.arange(lp, dtype=np.float64) - PAD)[:, None] * inv[None, :]
    c, s = np.cos(ang), np.sin(ang)
    return (jnp.asarray(np.concatenate([c, c], axis=-1), F32),
            jnp.asarray(np.concatenate([-s, s], axis=-1), F32))


def _row_tile(lp):
    for tm in (640, 512, 384, 256, 128):
        if lp % tm == 0:
            return tm
    raise ValueError(f"padded length {lp} is not a multiple of {CH}")


def kernel(x, meta_tokens, pre_mix_norm, w_in, gla_gate_w2, gla_gate_b, ret_norm_w, gla_norm_w, w_out,
           post_mix_norm, pre_ffn_norm, ffn_up, ffn_conv_w, ffn_conv_b, ffn_down, post_ffn_norm):
    bsz, seq, d = x.shape
    depth = w_in.shape[0]
    assert d == D_MODEL and seq % CH == 0
    lp = CH + seq
    tm = _row_tile(lp)

    cos2, sin2 = _rope_tables(lp)
    consts = _mixer_constants(tm)

    vec = lambda v: v.reshape(depth, 1, -1).astype(F32)
    w_b = w_in.astype(BF16)
    w_ga = w_b[:, :, MAIN_W:]
    w_ga3 = jnp.concatenate(
        [w_ga, w_ga, w_ga, jnp.zeros((depth, d, LANES - 3 * GLA_GATE_RANK), BF16)], axis=2)
    w2 = gla_gate_w2.astype(F32)
    w2_hi = w2.astype(BF16)
    w2_lo = (w2 - w2_hi.astype(F32)).astype(BF16)
    w2cat = jnp.concatenate(
        [w2_hi, w2_hi, w2_lo, jnp.zeros((depth, LANES - 3 * GLA_GATE_RANK, GLA_QK), BF16)], axis=1)
    pre_mix, gate_b, ret_nw, gla_nw = vec(pre_mix_norm), vec(gla_gate_b), vec(ret_norm_w), vec(gla_norm_w)
    ffn_w = (w_out.astype(F32), ffn_up.astype(F32), ffn_down.astype(F32))
    post_mix, pre_ffn, post_ffn = vec(post_mix_norm), vec(pre_ffn_norm), vec(post_ffn_norm)
    conv_w, conv_b = ffn_conv_w.astype(F32), vec(ffn_conv_b)
    h = x.astype(F32)
    for l in range(depth):
        outs = _proj(h, meta_tokens.astype(F32) if l == 0 else None, pre_mix, w_b, w_ga3, w2cat, gate_b,
                     cos2, sin2, consts["zeta_t"], ffn_w, l, tm)
        if l == 0:
            h = outs[-1]
        wo_b, up_b, dn_b = outs[5:8]
        merged = _mixer(*outs[:5], ret_nw, gla_nw, consts, l)
        h = _ffn(merged, h, wo_b, post_mix, pre_ffn, up_b, conv_w, conv_b, dn_b, post_ffn, l, tm,
                 last=(l == depth - 1))
    return h
```

```python
import functools

import numpy as np
import jax
import jax.numpy as jnp
from jax import lax
from jax.experimental import pallas as pl
from jax.experimental.pallas import tpu as pltpu

D_MODEL = 1024
N_META = 16
RET_HEADS = 4
RET_DK = 128
RET_DV = 128
GLA_HEADS = 4
GLA_DK = 64
GLA_DV = 128
GLA_GATE_RANK = 16
GLA_TAU = 16.0
D_FF = 2816
CONV_W = 3
ROPE_BASE = 10000.0
EPS = 1e-6

RET_QK = RET_HEADS * RET_DK
RET_V = RET_HEADS * RET_DV
GLA_QK = GLA_HEADS * GLA_DK
GLA_V = GLA_HEADS * GLA_DV
D_MIX = RET_V + GLA_V
IN_WIDTH = 2 * RET_QK + 2 * RET_V + 2 * GLA_QK + 2 * GLA_V + GLA_GATE_RANK

LANES = 128
SUBLANES = 8
CH = 128
PAD = CH - N_META
C_RQ, C_RK, C_RV, C_RG = 0, RET_QK, 2 * RET_QK, 2 * RET_QK + RET_V
C_GQ = C_RG + RET_V
C_GK, C_GV = C_GQ + GLA_QK, C_GQ + 2 * GLA_QK
C_GR = C_GV + GLA_V
MAIN_W = C_GR + GLA_V
FF_TILE = 256
FFN_PARTS = 2
VMEM_LIMIT = 56 * 1024 * 1024
LOG2E = float(np.log2(np.e))

BF16 = jnp.bfloat16
F32 = jnp.float32

_NT = (((1,), (1,)), ((), ()))
_TN = (((0,), (0,)), ((), ()))


def _rms(x):
    return x * lax.rsqrt(jnp.mean(x * x, axis=-1, keepdims=True) + EPS)


def _silu(g):
    return g * (1.0 / (1.0 + jnp.exp(-g)))


def _const_spec(shape):
    nd = len(shape)
    return pl.BlockSpec(shape, lambda *_: (0,) * nd, pipeline_mode=pl.Buffered(1))


def _layer_spec(arr, l):
    nd = arr.ndim - 1
    return pl.BlockSpec((None,) + arr.shape[1:], lambda *_: (l,) + (0,) * nd, pipeline_mode=pl.Buffered(1))


def _proj_tile(tm, h_ref, nw_ref, w_ref, wga_ref, w2_ref, gb_ref, cos_ref, sin_ref, zeta_ref,
               wo_f32_ref, up_f32_ref, dn_f32_ref,
               ret_ref, gate_ref, gqk_ref, gv_ref, la_ref, wo_b16_ref, up_b16_ref, dn_b16_ref):
    j = pl.program_id(1)
    hm = tm // 2
    parts = (slice(0, hm), slice(hm, tm))
    hn = {}

    def norm(rs):
        hn[rs.start] = (_rms(h_ref[0, rs]) * nw_ref[...]).astype(BF16)

    def cols(rs, c0, width):
        return jnp.dot(hn[rs.start], w_ref[:, c0:c0 + width], preferred_element_type=F32)

    def rope(rs, t, h):
        x = t[:, h * RET_DK:(h + 1) * RET_DK]
        return x * cos_ref[rs, :] + pltpu.roll(x, RET_DK // 2, 1) * sin_ref[rs, :]

    def put_q(rs, q):
        for h in range(RET_HEADS):
            ret_ref[0, rs, h * RET_DK:(h + 1) * RET_DK] = rope(rs, q, h).astype(BF16)

    def put_k(rs, k):
        for h in range(RET_HEADS):
            sl = slice(h * RET_DK, (h + 1) * RET_DK)
            kr = rope(rs, k, h)
            ret_ref[0, rs, RET_QK + h * RET_DK:RET_QK + (h + 1) * RET_DK] = kr.astype(BF16)
            ret_ref[0, rs, 2 * RET_QK + h * RET_DK:2 * RET_QK + (h + 1) * RET_DK] = (
                kr * zeta_ref[rs, sl]).astype(BF16)

    def put_v(rs, v):
        ret_ref[0, rs, 3 * RET_QK:] = v.astype(BF16)

    def put_rg(rs, g):
        gate_ref[0, rs, :RET_V] = _silu(g)

    def put_gr(rs, g):
        gate_ref[0, rs, RET_V:] = _silu(g)

    def put_gqk(rs, t):
        gqk_ref[0, rs, :GLA_QK] = t[:, :GLA_QK] * (GLA_DK ** -0.5)
        gqk_ref[0, rs, GLA_QK:] = t[:, GLA_QK:]

    def put_gv(rs, v):
        gv_ref[0, rs] = v.astype(BF16)

    def put_la(rs, g):
        g_hi = g.astype(BF16)
        g_lo = (g - g_hi.astype(F32)).astype(BF16)
        lane = lax.broadcasted_iota(jnp.int32, (1, LANES), 1)
        lo_part = (lane >= GLA_GATE_RANK) & (lane < 2 * GLA_GATE_RANK)
        z = jnp.dot(jnp.where(lo_part, g_lo, g_hi), w2_ref[...], preferred_element_type=F32) + gb_ref[...]
        la = (jnp.minimum(z, 0.0) - jnp.log(1.0 + jnp.exp(-jnp.abs(z)))) * (LOG2E / GLA_TAU)
        row = j * tm + rs.start + lax.broadcasted_iota(jnp.int32, (hm, 1), 0)
        la_ref[0, rs] = jnp.where(row >= PAD, la, 0.0)

    def cast_weights():
        for src, dst in ((wo_f32_ref, wo_b16_ref), (up_f32_ref, up_b16_ref), (dn_f32_ref, dn_b16_ref)):
            dst[...] = src[...].astype(BF16)

    def stages(rs):
        return [
            (lambda: jnp.dot(hn[rs.start], wga_ref[...], preferred_element_type=F32), put_la),
            (lambda: cols(rs, C_RQ, RET_QK), put_q),
            (lambda: cols(rs, C_RK, RET_QK), put_k),
            (lambda: cols(rs, C_RG, RET_V), put_rg),
            (lambda: cols(rs, C_GR, GLA_V), put_gr),
            (lambda: cols(rs, C_GQ, 2 * GLA_QK), put_gqk),
            (lambda: cols(rs, C_RV, RET_V), put_v),
            (lambda: cols(rs, C_GV, GLA_V), put_gv),
        ]

    per_part = [stages(rs) for rs in parts]
    order = [(rs, st) for group in zip(*per_part) for rs, st in zip(parts, group)]
    pending = None
    for rs, (matmul, epilogue) in order:
        if rs.start not in hn:
            norm(rs)
        res = matmul()
        if pending is not None:
            pending()
        pending = functools.partial(epilogue, rs, res)
    cast_weights()
    pending()


def _proj_kernel(tm, *refs):
    _proj_tile(tm, *refs)


def _proj_first_kernel(tm, x_ref, meta_ref, *refs):
    h_ref = refs[-1]
    j = pl.program_id(1)

    @pl.when(j == 0)
    def _():
        h_ref[0, :PAD] = jnp.zeros((PAD, D_MODEL), F32)
        h_ref[0, PAD:CH] = meta_ref[...]
        h_ref[0, CH:] = x_ref[:tm - CH]

    @pl.when(j > 0)
    def _():
        h_ref[0] = x_ref[...]

    _proj_tile(tm, h_ref, *refs[:-1])


def _x_window(tm, d):
    return pl.BlockSpec((None, pl.Element(tm), pl.Element(d)),
                        lambda i, j: (i, pl.multiple_of(jnp.maximum(j * tm - CH, 0), CH), 0))


def _slab_specs(arr, l, axis, unit, steps, nt):
    n_units = arr.shape[axis] // unit
    n_slabs = max(s for s in range(1, n_units + 1) if n_units % s == 0 and s <= steps)
    size = arr.shape[axis] // n_slabs
    shape = tuple(size if a == axis else arr.shape[a] for a in (1, 2))

    def pos(i, j):
        s = jnp.minimum(i * nt + j, n_slabs - 1)
        return (s, 0) if axis == 1 else (0, s)

    return (pl.BlockSpec((None,) + shape, lambda i, j: (l,) + pos(i, j)),
            pl.BlockSpec(shape, pos),
            jax.ShapeDtypeStruct(arr.shape[1:], BF16))


def _proj(h_or_x, meta, nw, w, wga, w2cat, gb, cos2, sin2, zeta_t, ffn_w, l, tm):
    first = meta is not None
    b, d = h_or_x.shape[0], h_or_x.shape[2]
    lp = h_or_x.shape[1] + (CH if first else 0)
    nt = lp // tm
    row = lambda width: pl.BlockSpec((1, tm, width), lambda i, j: (i, j, 0))
    tab = pl.BlockSpec((tm, RET_DK), lambda i, j: (j, 0))
    w_main = pl.BlockSpec((None, d, MAIN_W), lambda i, j: (l, 0, 0), pipeline_mode=pl.Buffered(1))
    slabs = [_slab_specs(ffn_w[0], l, 1, 4 * SUBLANES, b * nt, nt),
             _slab_specs(ffn_w[1], l, 2, 2 * LANES, b * nt, nt),
             _slab_specs(ffn_w[2], l, 1, 4 * SUBLANES, b * nt, nt)]
    lead = [_x_window(tm, d), _const_spec(meta.shape)] if first else [row(d)]
    in_specs = lead + [_layer_spec(nw, l), w_main, _layer_spec(wga, l), _layer_spec(w2cat, l), _layer_spec(gb, l),
                       tab, tab, _const_spec(zeta_t.shape)] + [s[0] for s in slabs]
    out_specs = ([row(4 * RET_QK), row(RET_V + GLA_V), row(2 * GLA_QK), row(GLA_V), row(GLA_QK)]
                 + [s[1] for s in slabs])
    out_shape = [jax.ShapeDtypeStruct((b, lp, 4 * RET_QK), BF16),
                 jax.ShapeDtypeStruct((b, lp, RET_V + GLA_V), F32),
                 jax.ShapeDtypeStruct((b, lp, 2 * GLA_QK), F32),
                 jax.ShapeDtypeStruct((b, lp, GLA_V), BF16),
                 jax.ShapeDtypeStruct((b, lp, GLA_QK), F32)] + [s[2] for s in slabs]
    if first:
        out_specs.append(row(d))
        out_shape.append(jax.ShapeDtypeStruct((b, lp, d), F32))
    args = ([h_or_x, meta] if first else [h_or_x]) + [nw, w, wga, w2cat, gb, cos2, sin2, zeta_t, *ffn_w]
    return pl.pallas_call(
        functools.partial(_proj_first_kernel if first else _proj_kernel, tm),
        grid=(b, nt),
        in_specs=in_specs,
        out_specs=out_specs,
        out_shape=out_shape,
        compiler_params=pltpu.CompilerParams(dimension_semantics=("arbitrary", "arbitrary"),
                                             vmem_limit_bytes=VMEM_LIMIT),
        name="proj_first" if first else "proj",
    )(*args)


def _level_halves():
    out, m = [], CH // 2
    while m >= 1:
        out.append(m)
        m //= 2
    return tuple(out)


def _mixer_constants(tm):
    c = CH
    idx = np.arange(c, dtype=np.float64)
    gam = 1.0 - 2.0 ** (-5.0 - np.arange(RET_HEADS, dtype=np.float64))
    kscale = RET_DK ** -0.5
    diff = idx[:, None] - idx[None, :]
    dmat = np.where(diff >= 0, gam[:, None, None] ** np.maximum(diff, 0.0), 0.0) * kscale
    zeta = np.repeat((gam[:, None] ** (c - 1.0 - idx)[None, :]).T, RET_DK, axis=1) * kscale
    xi = np.repeat((gam[:, None] ** (idx + 1.0)[None, :]).T, RET_DK, axis=1)
    gch = tuple(float(g ** c) for g in gam)
    s = np.arange(c)
    ml = [np.eye(c)]
    for m in _level_halves():
        blk = s // (2 * m)
        upper = (s % (2 * m)) >= m
        ml.append(((blk[:, None] == blk[None, :]) & upper[:, None] & (~upper)[None, :]).astype(np.float64))
    mlev = np.stack(ml, axis=0)
    tril = np.tril(np.ones((c, c)))
    return dict(dmat=jnp.asarray(dmat, F32), zeta_t=jnp.asarray(np.tile(zeta, (tm // c, 1)), F32),
                xi=jnp.asarray(xi, F32), gch=gch, tril=jnp.asarray(tril, BF16), mlev=jnp.asarray(mlev, F32))


def _block_ref(cum, m):
    parts = []
    for r0 in range(0, CH, 2 * m):
        parts.append(jnp.broadcast_to(cum[r0 + m - 1:r0 + m, :], (2 * m, cum.shape[1])))
    return parts[0] if len(parts) == 1 else jnp.concatenate(parts, axis=0)


def _mixer_kernel(gch, nb,
                  ret_ref, gate_ref, gqk_ref, gv_ref, la_ref, rnw_ref, gnw_ref,
                  dmat_ref, xi_ref, tril_ref, mlev_ref,
                  out_ref, sret_ref, sgla_ref):
    j = pl.program_id(0)

    @pl.when(j == 0)
    def _():
        sret_ref[...] = jnp.zeros_like(sret_ref)
        sgla_ref[...] = jnp.zeros_like(sgla_ref)

    row = lax.broadcasted_iota(jnp.int32, (CH, 1), 0)
    lane = lax.broadcasted_iota(jnp.int32, (1, GLA_QK), 1)
    even = (lane % LANES) < GLA_DK
    tril = tril_ref[...]

    def retention(b, h):
        sl = slice(h * RET_DK, (h + 1) * RET_DK)
        q = ret_ref[b, :, sl]
        k = ret_ref[b, :, RET_QK + h * RET_DK:RET_QK + (h + 1) * RET_DK]
        kz = ret_ref[b, :, 2 * RET_QK + h * RET_DK:2 * RET_QK + (h + 1) * RET_DK]
        vb = ret_ref[b, :, 3 * RET_QK + h * RET_DV:3 * RET_QK + (h + 1) * RET_DV]
        sc = lax.dot_general(q, k, _NT, preferred_element_type=F32) * dmat_ref[h]
        st = sret_ref[b, h]
        o = (jnp.dot(sc.astype(BF16), vb, preferred_element_type=F32)
             + jnp.dot(q, st.astype(BF16), preferred_element_type=F32) * xi_ref[:, sl])
        sret_ref[b, h] = st * gch[h] + lax.dot_general(kz, vb, _TN, preferred_element_type=F32)
        d = o - jnp.mean(o, axis=-1, keepdims=True)
        y = d * lax.rsqrt(jnp.mean(d * d, axis=-1, keepdims=True) + EPS) * rnw_ref[:, sl]
        out_ref[b, :, sl] = (y * gate_ref[b, :, sl]).astype(out_ref.dtype)

    def gla_prepare(b):
        la = la_ref[b]
        la_hi = la.astype(BF16)
        r1 = la - la_hi.astype(F32)
        la_mid = r1.astype(BF16)
        la_lo = (r1 - la_mid.astype(F32)).astype(BF16)
        cum = (jnp.dot(tril, la_hi, preferred_element_type=F32)
               + jnp.dot(tril, la_mid, preferred_element_type=F32)
               + jnp.dot(tril, la_lo, preferred_element_type=F32))
        qg = gqk_ref[b, :, :GLA_QK]
        kg = gqk_ref[b, :, GLA_QK:]
        return dict(la=la, cum=cum, kg=kg, qg_e=jnp.where(even, qg, 0.0), qg_o=jnp.where(even, 0.0, qg),
                    la_up=pltpu.roll(la, CH - 1, 0), la_dn=pltpu.roll(la, 1, 0),
                    amat=[None] * GLA_HEADS)

    levels = (0,) + _level_halves()

    def gla_level(p, pr, lv):
        pls = slice(pr * LANES, (pr + 1) * LANES)
        la, cum, kg, qg_e, qg_o = (p[n][:, pls] for n in ("la", "cum", "kg", "qg_e", "qg_o"))
        m = levels[lv]
        if m == 0:
            qe, qo, kh = qg_e.astype(BF16), qg_o.astype(BF16), kg.astype(BF16)
        else:
            if m >= SUBLANES // 2:
                ex = -jnp.abs(cum - _block_ref(cum, m))
            elif m == 2:
                pos = row % 4
                ex = jnp.where(pos == 0, p["la_up"][:, pls],
                               jnp.where(pos == 2, la, jnp.where(pos == 3, la + p["la_dn"][:, pls], 0.0)))
            else:
                ex = jnp.where(row % 2 == 1, la, 0.0)
            e = jnp.exp2(ex)
            qe, qo, kh = (qg_e * e).astype(BF16), (qg_o * e).astype(BF16), (kg * e).astype(BF16)
        msk = mlev_ref[lv]
        sc = lax.dot_general(jnp.concatenate([qe, qo], axis=0), kh, _NT, preferred_element_type=F32)
        for t in range(2):
            term = sc[t * CH:(t + 1) * CH] * msk
            p["amat"][2 * pr + t] = term if lv == 0 else p["amat"][2 * pr + t] + term

    def gla_decays(p):
        cum, kg = p["cum"], p["kg"]
        last = cum[CH - 1:CH, :]
        ecum = jnp.exp2(cum)
        kd = kg * jnp.exp2(last - cum)
        p.update(qd=((p["qg_e"] * ecum).astype(BF16), (p["qg_o"] * ecum).astype(BF16)),
                 kd=(jnp.where(even, kd, 0.0).astype(BF16), jnp.where(even, 0.0, kd).astype(BF16)),
                 dl=jnp.exp2(last))

    def gla_finish(b, p, pr):
        pls = slice(pr * LANES, (pr + 1) * LANES)
        st = sgla_ref[b, :, pls]
        stb = st.astype(BF16)
        vbs = []
        for t in range(2):
            h = 2 * pr + t
            sl = slice(h * GLA_DV, (h + 1) * GLA_DV)
            osl = slice(RET_V + h * GLA_DV, RET_V + (h + 1) * GLA_DV)
            vb = gv_ref[b, :, sl]
            vbs.append(vb)
            o = (jnp.dot(p["amat"][h].astype(BF16), vb, preferred_element_type=F32)
                 + lax.dot_general(p["qd"][t][:, pls], stb, _NT, preferred_element_type=F32))
            y = o * lax.rsqrt(jnp.mean(o * o, axis=-1, keepdims=True) + EPS) * gnw_ref[:, sl]
            out_ref[b, :, osl] = (y * gate_ref[b, :, osl]).astype(out_ref.dtype)
        kvt = lax.dot_general(jnp.concatenate(vbs, axis=0),
                              jnp.concatenate([p["kd"][0][:, pls], p["kd"][1][:, pls]], axis=0),
                              _TN, preferred_element_type=F32)
        sgla_ref[b, :, pls] = st * p["dl"][:, pls] + kvt

    preps = [gla_prepare(b) for b in range(nb)]
    fillers = []
    for b in range(nb):
        fillers += [functools.partial(retention, b, h) for h in range(RET_HEADS)]
    for b in range(nb):
        for pr in range(GLA_HEADS // 2):
            for lv in range(len(levels)):
                gla_level(preps[b], pr, lv)
                if fillers and lv % 2 == 1:
                    fillers.pop(0)()
        gla_decays(preps[b])
        fillers += [functools.partial(gla_finish, b, preps[b], pr) for pr in range(GLA_HEADS // 2)]
    for f in fillers:
        f()


def _mixer(ret, gate, gqk, gv, la, rnw, gnw, consts, l):
    b, lp, _ = gate.shape

    def blk(arr):
        return pl.BlockSpec((b, CH, arr.shape[2]), lambda j: (0, j, 0))

    tabs = [consts[k] for k in ("dmat", "xi", "tril", "mlev")]
    return pl.pallas_call(
        functools.partial(_mixer_kernel, consts["gch"], b),
        grid=(lp // CH,),
        in_specs=[blk(ret), blk(gate), blk(gqk), blk(gv), blk(la), _layer_spec(rnw, l), _layer_spec(gnw, l)]
        + [_const_spec(t.shape) for t in tabs],
        out_specs=pl.BlockSpec((b, CH, D_MIX), lambda j: (0, j, 0)),
        out_shape=jax.ShapeDtypeStruct((b, lp, D_MIX), BF16),
        scratch_shapes=[pltpu.VMEM((b, RET_HEADS, RET_DK, RET_DV), F32),
                        pltpu.VMEM((b, GLA_DV, GLA_QK), F32)],
        compiler_params=pltpu.CompilerParams(dimension_semantics=("arbitrary",),
                                             vmem_limit_bytes=VMEM_LIMIT),
        name="mixer",
    )(ret, gate, gqk, gv, la, rnw, gnw, *tabs)


def _gelu_tanh(x):
    return 0.5 * x * (1.0 + jnp.tanh(np.sqrt(2.0 / np.pi) * (x + 0.044715 * (x * x * x))))


def _ffn_part_bounds(tm):
    unit = 2 * SUBLANES
    n_units = tm // unit
    sizes = [(n_units // FFN_PARTS + (1 if t < n_units % FFN_PARTS else 0)) * unit for t in range(FFN_PARTS)]
    return [sum(sizes[:t]) for t in range(FFN_PARTS + 1)]


def _ffn_tile(tm, m_ref, h_ref, wout_ref, pmn_ref, pfn_ref, up_ref, cw_ref, cb_ref, down_ref, pon_ref,
              halo_ref, act_ref, f_ref):
    i = pl.program_id(1)

    @pl.when(i == 0)
    def _():
        halo_ref[0:8] = jnp.zeros((8, 2 * D_FF), F32)

    bounds = _ffn_part_bounds(tm)
    parts = [slice(bounds[t], bounds[t + 1]) for t in range(FFN_PARTS)]
    sub = lax.broadcasted_iota(jnp.int32, (8, 1), 0)

    m = [jnp.dot(m_ref[0, rows], wout_ref[...], preferred_element_type=F32) for rows in parts]
    h1, hn = [], []
    for t in range(FFN_PARTS):
        x = h_ref[0, parts[t]] + _rms(m[t]) * pmn_ref[...]
        h1.append(x)
        hn.append((_rms(x) * pfn_ref[...]).astype(BF16))

    def conv(t, c0):
        cs = slice(c0, c0 + FF_TILE)
        u = jnp.dot(hn[t], up_ref[:, cs], preferred_element_type=F32)
        prev = halo_ref[8 * t:8 * t + 8, cs]
        nxt = 8 * ((t + 1) % FFN_PARTS)
        halo_ref[nxt:nxt + 8, cs] = u[u.shape[0] - 8:]
        s1 = pltpu.roll(u, 1, 0)
        s2 = pltpu.roll(u, 2, 0)
        top1 = jnp.where(sub < 1, pltpu.roll(prev, 1, 0), s1[:8])
        top2 = jnp.where(sub < 2, pltpu.roll(prev, 2, 0), s2[:8])
        s1 = jnp.concatenate([top1, s1[8:]], axis=0)
        s2 = jnp.concatenate([top2, s2[8:]], axis=0)
        return (cb_ref[:, cs] + u * cw_ref[2:3, cs] + s1 * cw_ref[1:2, cs] + s2 * cw_ref[0:1, cs])

    def act_chunk(t, c0):
        a = conv(t, c0)
        g = conv(t, D_FF + c0)
        act_ref[parts[t], c0:c0 + FF_TILE] = (_gelu_tanh(a) * g).astype(BF16)

    def down_piece(t, p0):
        f_ref[parts[t], p0:p0 + 2 * LANES] = jnp.dot(act_ref[parts[t], :], down_ref[:, p0:p0 + 2 * LANES],
                                                      preferred_element_type=F32)

    chunks = list(range(0, D_FF, FF_TILE))
    for t in range(FFN_PARTS):
        pieces = list(range(0, D_MODEL, 2 * LANES)) if t > 0 else []
        for n, c0 in enumerate(chunks):
            act_chunk(t, c0)
            if pieces and n % 2 == 1:
                down_piece(t - 1, pieces.pop(0))
        for p0 in pieces:
            down_piece(t - 1, p0)
    last = FFN_PARTS - 1
    f_last = jnp.dot(act_ref[parts[last], :], down_ref[...], preferred_element_type=F32)
    h2 = jnp.concatenate([h1[t] + _rms(f_ref[parts[t], :]) * pon_ref[...] for t in range(last)]
                         + [h1[last] + _rms(f_last) * pon_ref[...]], axis=0)
    row = i * tm + lax.broadcasted_iota(jnp.int32, (tm, 1), 0)
    return jnp.where(row >= PAD, h2, 0.0)


def _ffn_kernel(tm, *refs):
    out_ref, halo_ref, act_ref, f_ref = refs[-4:]
    out_ref[0] = _ffn_tile(tm, *refs[:-4], halo_ref, act_ref, f_ref)


def _ffn_last_kernel(tm, *refs):
    out_ref, halo_ref, act_ref, f_ref, obuf_ref, osem = refs[-6:]
    b, i = pl.program_id(0), pl.program_id(1)
    nt = pl.num_programs(1)
    step = b * nt + i
    slot = step % 2

    def first_copy(s, bb):
        return pltpu.make_async_copy(obuf_ref.at[s, pl.ds(CH, tm - CH)],
                                     out_ref.at[bb, pl.ds(0, tm - CH)], osem.at[s])

    def tile_copy(s, bb, ii):
        return pltpu.make_async_copy(obuf_ref.at[s], out_ref.at[bb, pl.ds(ii * tm - CH, tm)], osem.at[s])

    def wait_step(s, bb, ii):
        @pl.when(ii == 0)
        def _():
            first_copy(s, bb).wait()

        @pl.when(ii > 0)
        def _():
            tile_copy(s, bb, ii).wait()

    res = _ffn_tile(tm, *refs[:-6], halo_ref, act_ref, f_ref)

    @pl.when(step >= 2)
    def _():
        prev = step - 2
        wait_step(slot, prev // nt, prev % nt)

    obuf_ref[slot] = res

    @pl.when(i == 0)
    def _():
        first_copy(slot, b).start()

    @pl.when(i > 0)
    def _():
        tile_copy(slot, b, i).start()

    @pl.when(step == pl.num_programs(0) * nt - 1)
    def _():
        wait_step(1 - slot, (step - 1) // nt, (step - 1) % nt)
        wait_step(slot, b, i)


def _ffn(merged, h, wout, pmn, pfn, up, cw, cb, down, pon, l, tm, last):
    b, lp, d = h.shape
    assert lp // tm >= 2
    row = lambda width: pl.BlockSpec((1, tm, width), lambda i, j: (i, j, 0))
    assert tm % (2 * SUBLANES) == 0 and tm // (2 * SUBLANES) >= FFN_PARTS
    scratch = [pltpu.VMEM((FFN_PARTS * SUBLANES, 2 * D_FF), F32), pltpu.VMEM((tm, D_FF), BF16),
               pltpu.VMEM((tm, d), F32)]
    if last:
        scratch += [pltpu.VMEM((2, tm, d), F32), pltpu.SemaphoreType.DMA((2,))]
    return pl.pallas_call(
        functools.partial(_ffn_last_kernel if last else _ffn_kernel, tm),
        grid=(b, lp // tm),
        in_specs=[row(D_MIX), row(d)] + [_const_spec(a.shape) if a.ndim == 2 else _layer_spec(a, l)
                                         for a in (wout, pmn, pfn, up, cw, cb, down, pon)],
        out_specs=pl.BlockSpec(memory_space=pl.ANY) if last else row(d),
        out_shape=jax.ShapeDtypeStruct((b, lp - CH, d) if last else (b, lp, d), F32),
        scratch_shapes=scratch,
        compiler_params=pltpu.CompilerParams(
            dimension_semantics=("arbitrary", "arbitrary") if last else ("parallel", "arbitrary"),
            vmem_limit_bytes=VMEM_LIMIT),
        name="ffn_last" if last else "ffn",
    )(merged, h, wout, pmn, pfn, up, cw, cb, down, pon)


def _rope_tables(lp):
    half = RET_DK // 2
    inv = ROPE_BASE ** (-np.arange(half, dtype=np.float64) / half)
    ang = (np.arange(lp, dtype=np.float64) - PAD)[:, None] * inv[None, :]
    c, s = np.cos(ang), np.sin(ang)
    return (jnp.asarray(np.concatenate([c, c], axis=-1), F32),
            jnp.asarray(np.concatenate([-s, s], axis=-1), F32))


def _row_tile(lp):
    for tm in (640, 512, 384, 256, 128):
        if lp % tm == 0:
            return tm
    raise ValueError(f"padded length {lp} is not a multiple of {CH}")


def kernel(x, meta_tokens, pre_mix_norm, w_in, gla_gate_w2, gla_gate_b, ret_norm_w, gla_norm_w, w_out,
           post_mix_norm, pre_ffn_norm, ffn_up, ffn_conv_w, ffn_conv_b, ffn_down, post_ffn_norm):
    bsz, seq, d = x.shape
    depth = w_in.shape[0]
    assert d == D_MODEL and seq % CH == 0
    lp = CH + seq
    tm = _row_tile(lp)

    cos2, sin2 = _rope_tables(lp)
    consts = _mixer_constants(tm)

    vec = lambda v: v.reshape(depth, 1, -1).astype(F32)
    w_b = w_in.astype(BF16)
    w_ga = w_b[:, :, MAIN_W:]
    w_ga3 = jnp.concatenate(
        [w_ga, w_ga, w_ga, jnp.zeros((depth, d, LANES - 3 * GLA_GATE_RANK), BF16)], axis=2)
    w2 = gla_gate_w2.astype(F32)
    w2_hi = w2.astype(BF16)
    w2_lo = (w2 - w2_hi.astype(F32)).astype(BF16)
    w2cat = jnp.concatenate(
        [w2_hi, w2_hi, w2_lo, jnp.zeros((depth, LANES - 3 * GLA_GATE_RANK, GLA_QK), BF16)], axis=1)
    pre_mix, gate_b, ret_nw, gla_nw = vec(pre_mix_norm), vec(gla_gate_b), vec(ret_norm_w), vec(gla_norm_w)
    ffn_w = (w_out.astype(F32), ffn_up.astype(F32), ffn_down.astype(F32))
    post_mix, pre_ffn, post_ffn = vec(post_mix_norm), vec(pre_ffn_norm), vec(post_ffn_norm)
    conv_w, conv_b = ffn_conv_w.astype(F32), vec(ffn_conv_b)
    h = x.astype(F32)
    for l in range(depth):
        outs = _proj(h, meta_tokens.astype(F32) if l == 0 else None, pre_mix, w_b, w_ga3, w2cat, gate_b,
                     cos2, sin2, consts["zeta_t"], ffn_w, l, tm)
        if l == 0:
            h = outs[-1]
        wo_b, up_b, dn_b = outs[5:8]
        merged = _mixer(*outs[:5], ret_nw, gla_nw, consts, l)
        h = _ffn(merged, h, wo_b, post_mix, pre_ffn, up_b, conv_w, conv_b, dn_b, post_ffn, l, tm,
                 last=(l == depth - 1))
    return h
```

```python
import functools

import numpy as np
import jax
import jax.numpy as jnp
from jax import lax
from jax.experimental import pallas as pl
from jax.experimental.pallas import tpu as pltpu

D_MODEL = 1024
N_META = 16
RET_HEADS = 4
RET_DK = 128
RET_DV = 128
GLA_HEADS = 4
GLA_DK = 64
GLA_DV = 128
GLA_GATE_RANK = 16
GLA_TAU = 16.0
D_FF = 2816
CONV_W = 3
ROPE_BASE = 10000.0
EPS = 1e-6

RET_QK = RET_HEADS * RET_DK
RET_V = RET_HEADS * RET_DV
GLA_QK = GLA_HEADS * GLA_DK
GLA_V = GLA_HEADS * GLA_DV
D_MIX = RET_V + GLA_V
IN_WIDTH = 2 * RET_QK + 2 * RET_V + 2 * GLA_QK + 2 * GLA_V + GLA_GATE_RANK

LANES = 128
SUBLANES = 8
CH = 128
PAD = CH - N_META
C_RQ, C_RK, C_RV, C_RG = 0, RET_QK, 2 * RET_QK, 2 * RET_QK + RET_V
C_GQ = C_RG + RET_V
C_GK, C_GV = C_GQ + GLA_QK, C_GQ + 2 * GLA_QK
C_GR = C_GV + GLA_V
MAIN_W = C_GR + GLA_V
MIXER_CHUNKS_PER_STEP = 5
FF_TILE = 256
FFN_PARTS = 2
VMEM_LIMIT = 56 * 1024 * 1024
LOG2E = float(np.log2(np.e))

BF16 = jnp.bfloat16
F32 = jnp.float32

_NT = (((1,), (1,)), ((), ()))
_TN = (((0,), (0,)), ((), ()))


def _rms(x):
    return x * lax.rsqrt(jnp.mean(x * x, axis=-1, keepdims=True) + EPS)


def _silu(g):
    return g * (1.0 / (1.0 + jnp.exp(-g)))


def _const_spec(shape):
    nd = len(shape)
    return pl.BlockSpec(shape, lambda *_: (0,) * nd, pipeline_mode=pl.Buffered(1))


def _layer_spec(arr, l):
    nd = arr.ndim - 1
    return pl.BlockSpec((None,) + arr.shape[1:], lambda *_: (l,) + (0,) * nd, pipeline_mode=pl.Buffered(1))


def _proj_tile(tm, h_ref, nw_ref, w_ref, wga_ref, w2_ref, gb_ref, cos_ref, sin_ref, zeta_ref,
               wo_f32_ref, up_f32_ref, dn_f32_ref,
               ret_ref, gate_ref, gqk_ref, gv_ref, la_ref, wo_b16_ref, up_b16_ref, dn_b16_ref):
    j = pl.program_id(1)
    hm = tm // 2
    parts = (slice(0, hm), slice(hm, tm))
    hn = {}

    def norm(rs):
        hn[rs.start] = (_rms(h_ref[0, rs]) * nw_ref[...]).astype(BF16)

    def cols(rs, c0, width):
        return jnp.dot(hn[rs.start], w_ref[:, c0:c0 + width], preferred_element_type=F32)

    def rope(rs, t, h):
        x = t[:, h * RET_DK:(h + 1) * RET_DK]
        return x * cos_ref[rs, :] + pltpu.roll(x, RET_DK // 2, 1) * sin_ref[rs, :]

    def put_q(rs, q):
        for h in range(RET_HEADS):
            ret_ref[0, rs, h * RET_DK:(h + 1) * RET_DK] = rope(rs, q, h).astype(BF16)

    def put_k(rs, k):
        for h in range(RET_HEADS):
            sl = slice(h * RET_DK, (h + 1) * RET_DK)
            kr = rope(rs, k, h)
            ret_ref[0, rs, RET_QK + h * RET_DK:RET_QK + (h + 1) * RET_DK] = kr.astype(BF16)
            ret_ref[0, rs, 2 * RET_QK + h * RET_DK:2 * RET_QK + (h + 1) * RET_DK] = (
                kr * zeta_ref[rs, sl]).astype(BF16)

    def put_v(rs, v):
        ret_ref[0, rs, 3 * RET_QK:] = v.astype(BF16)

    def put_rg(rs, g):
        gate_ref[0, rs, :RET_V] = _silu(g)

    def put_gr(rs, g):
        gate_ref[0, rs, RET_V:] = _silu(g)

    def put_gqk(rs, t):
        gqk_ref[0, rs, :GLA_QK] = t[:, :GLA_QK] * (GLA_DK ** -0.5)
        gqk_ref[0, rs, GLA_QK:] = t[:, GLA_QK:]

    def put_gv(rs, v):
        gv_ref[0, rs] = v.astype(BF16)

    def put_la(rs, g):
        g_hi = g.astype(BF16)
        g_lo = (g - g_hi.astype(F32)).astype(BF16)
        lane = lax.broadcasted_iota(jnp.int32, (1, LANES), 1)
        lo_part = (lane >= GLA_GATE_RANK) & (lane < 2 * GLA_GATE_RANK)
        z = jnp.dot(jnp.where(lo_part, g_lo, g_hi), w2_ref[...], preferred_element_type=F32) + gb_ref[...]
        la = (jnp.minimum(z, 0.0) - jnp.log(1.0 + jnp.exp(-jnp.abs(z)))) * (LOG2E / GLA_TAU)
        row = j * tm + rs.start + lax.broadcasted_iota(jnp.int32, (hm, 1), 0)
        la_ref[0, rs] = jnp.where(row >= PAD, la, 0.0)

    def cast_weights():
        for src, dst in ((wo_f32_ref, wo_b16_ref), (up_f32_ref, up_b16_ref), (dn_f32_ref, dn_b16_ref)):
            dst[...] = src[...].astype(BF16)

    def stages(rs):
        return [
            (lambda: jnp.dot(hn[rs.start], wga_ref[...], preferred_element_type=F32), put_la),
            (lambda: cols(rs, C_RQ, RET_QK), put_q),
            (lambda: cols(rs, C_RK, RET_QK), put_k),
            (lambda: cols(rs, C_RG, RET_V), put_rg),
            (lambda: cols(rs, C_GR, GLA_V), put_gr),
            (lambda: cols(rs, C_GQ, 2 * GLA_QK), put_gqk),
            (lambda: cols(rs, C_RV, RET_V), put_v),
            (lambda: cols(rs, C_GV, GLA_V), put_gv),
        ]

    per_part = [stages(rs) for rs in parts]
    order = [(rs, st) for group in zip(*per_part) for rs, st in zip(parts, group)]
    pending = None
    for rs, (matmul, epilogue) in order:
        if rs.start not in hn:
            norm(rs)
        res = matmul()
        if pending is not None:
            pending()
        pending = functools.partial(epilogue, rs, res)
    cast_weights()
    pending()


def _proj_kernel(tm, *refs):
    _proj_tile(tm, *refs)


def _proj_first_kernel(tm, x_ref, meta_ref, *refs):
    h_ref = refs[-1]
    j = pl.program_id(1)

    @pl.when(j == 0)
    def _():
        h_ref[0, :PAD] = jnp.zeros((PAD, D_MODEL), F32)
        h_ref[0, PAD:CH] = meta_ref[...]
        h_ref[0, CH:] = x_ref[:tm - CH]

    @pl.when(j > 0)
    def _():
        h_ref[0] = x_ref[...]

    _proj_tile(tm, h_ref, *refs[:-1])


def _x_window(tm, d):
    return pl.BlockSpec((None, pl.Element(tm), pl.Element(d)),
                        lambda i, j: (i, pl.multiple_of(jnp.maximum(j * tm - CH, 0), CH), 0))


def _slab_specs(arr, l, axis, unit, steps, nt):
    n_units = arr.shape[axis] // unit
    n_slabs = max(s for s in range(1, n_units + 1) if n_units % s == 0 and s <= steps)
    size = arr.shape[axis] // n_slabs
    shape = tuple(size if a == axis else arr.shape[a] for a in (1, 2))

    def pos(i, j):
        s = jnp.minimum(i * nt + j, n_slabs - 1)
        return (s, 0) if axis == 1 else (0, s)

    return (pl.BlockSpec((None,) + shape, lambda i, j: (l,) + pos(i, j)),
            pl.BlockSpec(shape, pos),
            jax.ShapeDtypeStruct(arr.shape[1:], BF16))


def _proj(h_or_x, meta, nw, w, wga, w2cat, gb, cos2, sin2, zeta_t, ffn_w, l, tm):
    first = meta is not None
    b, d = h_or_x.shape[0], h_or_x.shape[2]
    lp = h_or_x.shape[1] + (CH if first else 0)
    nt = lp // tm
    row = lambda width: pl.BlockSpec((1, tm, width), lambda i, j: (i, j, 0))
    tab = pl.BlockSpec((tm, RET_DK), lambda i, j: (j, 0))
    w_main = pl.BlockSpec((None, d, MAIN_W), lambda i, j: (l, 0, 0), pipeline_mode=pl.Buffered(1))
    slabs = [_slab_specs(ffn_w[0], l, 1, 4 * SUBLANES, b * nt, nt),
             _slab_specs(ffn_w[1], l, 2, 2 * LANES, b * nt, nt),
             _slab_specs(ffn_w[2], l, 1, 4 * SUBLANES, b * nt, nt)]
    lead = [_x_window(tm, d), _const_spec(meta.shape)] if first else [row(d)]
    in_specs = lead + [_layer_spec(nw, l), w_main, _layer_spec(wga, l), _layer_spec(w2cat, l), _layer_spec(gb, l),
                       tab, tab, _const_spec(zeta_t.shape)] + [s[0] for s in slabs]
    out_specs = ([row(4 * RET_QK), row(RET_V + GLA_V), row(2 * GLA_QK), row(GLA_V), row(GLA_QK)]
                 + [s[1] for s in slabs])
    out_shape = [jax.ShapeDtypeStruct((b, lp, 4 * RET_QK), BF16),
                 jax.ShapeDtypeStruct((b, lp, RET_V + GLA_V), F32),
                 jax.ShapeDtypeStruct((b, lp, 2 * GLA_QK), F32),
                 jax.ShapeDtypeStruct((b, lp, GLA_V), BF16),
                 jax.ShapeDtypeStruct((b, lp, GLA_QK), F32)] + [s[2] for s in slabs]
    if first:
        out_specs.append(row(d))
        out_shape.append(jax.ShapeDtypeStruct((b, lp, d), F32))
    args = ([h_or_x, meta] if first else [h_or_x]) + [nw, w, wga, w2cat, gb, cos2, sin2, zeta_t, *ffn_w]
    return pl.pallas_call(
        functools.partial(_proj_first_kernel if first else _proj_kernel, tm),
        grid=(b, nt),
        in_specs=in_specs,
        out_specs=out_specs,
        out_shape=out_shape,
        compiler_params=pltpu.CompilerParams(dimension_semantics=("arbitrary", "arbitrary"),
                                             vmem_limit_bytes=VMEM_LIMIT),
        name="proj_first" if first else "proj",
    )(*args)


def _level_halves():
    out, m = [], CH // 2
    while m >= 1:
        out.append(m)
        m //= 2
    return tuple(out)


def _mixer_constants(tm):
    c = CH
    idx = np.arange(c, dtype=np.float64)
    gam = 1.0 - 2.0 ** (-5.0 - np.arange(RET_HEADS, dtype=np.float64))
    kscale = RET_DK ** -0.5
    diff = idx[:, None] - idx[None, :]
    dmat = np.where(diff >= 0, gam[:, None, None] ** np.maximum(diff, 0.0), 0.0) * kscale
    zeta = np.repeat((gam[:, None] ** (c - 1.0 - idx)[None, :]).T, RET_DK, axis=1) * kscale
    xi = np.repeat((gam[:, None] ** (idx + 1.0)[None, :]).T, RET_DK, axis=1)
    gch = tuple(float(g ** c) for g in gam)
    s = np.arange(c)
    ml = [np.eye(c)]
    for m in _level_halves():
        blk = s // (2 * m)
        upper = (s % (2 * m)) >= m
        ml.append(((blk[:, None] == blk[None, :]) & upper[:, None] & (~upper)[None, :]).astype(np.float64))
    mlev = np.stack(ml, axis=0)
    tril = np.tril(np.ones((c, c)))
    return dict(dmat=jnp.asarray(dmat, F32), zeta_t=jnp.asarray(np.tile(zeta, (tm // c, 1)), F32),
                xi=jnp.asarray(xi, F32), gch=gch, tril=jnp.asarray(tril, BF16), mlev=jnp.asarray(mlev, F32))


def _block_ref(cum, m):
    parts = []
    for r0 in range(0, CH, 2 * m):
        parts.append(jnp.broadcast_to(cum[r0 + m - 1:r0 + m, :], (2 * m, cum.shape[1])))
    return parts[0] if len(parts) == 1 else jnp.concatenate(parts, axis=0)


def _mixer_kernel(gch, nb, n_sub,
                  ret_ref, gate_ref, gqk_ref, gv_ref, la_ref, rnw_ref, gnw_ref,
                  dmat_ref, xi_ref, tril_ref, mlev_ref,
                  out_ref, sret_ref, sgla_ref):
    j = pl.program_id(0)

    @pl.when(j == 0)
    def _():
        sret_ref[...] = jnp.zeros_like(sret_ref)
        sgla_ref[...] = jnp.zeros_like(sgla_ref)

    row = lax.broadcasted_iota(jnp.int32, (CH, 1), 0)
    lane = lax.broadcasted_iota(jnp.int32, (1, GLA_QK), 1)
    even = (lane % LANES) < GLA_DK
    tril = tril_ref[...]

    def retention(rs, b, h):
        sl = slice(h * RET_DK, (h + 1) * RET_DK)
        q = ret_ref[b, rs,sl]
        k = ret_ref[b, rs,RET_QK + h * RET_DK:RET_QK + (h + 1) * RET_DK]
        kz = ret_ref[b, rs,2 * RET_QK + h * RET_DK:2 * RET_QK + (h + 1) * RET_DK]
        vb = ret_ref[b, rs,3 * RET_QK + h * RET_DV:3 * RET_QK + (h + 1) * RET_DV]
        sc = lax.dot_general(q, k, _NT, preferred_element_type=F32) * dmat_ref[h]
        st = sret_ref[b, h]
        o = (jnp.dot(sc.astype(BF16), vb, preferred_element_type=F32)
             + jnp.dot(q, st.astype(BF16), preferred_element_type=F32) * xi_ref[:, sl])
        sret_ref[b, h] = st * gch[h] + lax.dot_general(kz, vb, _TN, preferred_element_type=F32)
        d = o - jnp.mean(o, axis=-1, keepdims=True)
        y = d * lax.rsqrt(jnp.mean(d * d, axis=-1, keepdims=True) + EPS) * rnw_ref[:, sl]
        out_ref[b, rs, sl] = (y * gate_ref[b, rs, sl]).astype(out_ref.dtype)

    def gla_prepare(rs, b):
        la = la_ref[b, rs]
        la_hi = la.astype(BF16)
        r1 = la - la_hi.astype(F32)
        la_mid = r1.astype(BF16)
        la_lo = (r1 - la_mid.astype(F32)).astype(BF16)
        cum = (jnp.dot(tril, la_hi, preferred_element_type=F32)
               + jnp.dot(tril, la_mid, preferred_element_type=F32)
               + jnp.dot(tril, la_lo, preferred_element_type=F32))
        qg = gqk_ref[b, rs,:GLA_QK]
        kg = gqk_ref[b, rs,GLA_QK:]
        return dict(la=la, cum=cum, kg=kg, qg_e=jnp.where(even, qg, 0.0), qg_o=jnp.where(even, 0.0, qg),
                    la_up=pltpu.roll(la, CH - 1, 0), la_dn=pltpu.roll(la, 1, 0),
                    amat=[None] * GLA_HEADS)

    levels = (0,) + _level_halves()

    def gla_level(p, pr, lv):
        pls = slice(pr * LANES, (pr + 1) * LANES)
        la, cum, kg, qg_e, qg_o = (p[n][:, pls] for n in ("la", "cum", "kg", "qg_e", "qg_o"))
        m = levels[lv]
        if m == 0:
            qe, qo, kh = qg_e.astype(BF16), qg_o.astype(BF16), kg.astype(BF16)
        else:
            if m >= SUBLANES // 2:
                ex = -jnp.abs(cum - _block_ref(cum, m))
            elif m == 2:
                pos = row % 4
                ex = jnp.where(pos == 0, p["la_up"][:, pls],
                               jnp.where(pos == 2, la, jnp.where(pos == 3, la + p["la_dn"][:, pls], 0.0)))
            else:
                ex = jnp.where(row % 2 == 1, la, 0.0)
            e = jnp.exp2(ex)
            qe, qo, kh = (qg_e * e).astype(BF16), (qg_o * e).astype(BF16), (kg * e).astype(BF16)
        msk = mlev_ref[lv]
        sc = lax.dot_general(jnp.concatenate([qe, qo], axis=0), kh, _NT, preferred_element_type=F32)
        for t in range(2):
            term = sc[t * CH:(t + 1) * CH] * msk
            p["amat"][2 * pr + t] = term if lv == 0 else p["amat"][2 * pr + t] + term

    def gla_decays(p):
        cum, kg = p["cum"], p["kg"]
        last = cum[CH - 1:CH, :]
        ecum = jnp.exp2(cum)
        kd = kg * jnp.exp2(last - cum)
        p.update(qd=((p["qg_e"] * ecum).astype(BF16), (p["qg_o"] * ecum).astype(BF16)),
                 kd=(jnp.where(even, kd, 0.0).astype(BF16), jnp.where(even, 0.0, kd).astype(BF16)),
                 dl=jnp.exp2(last))

    def gla_finish(rs, b, p, pr):
        pls = slice(pr * LANES, (pr + 1) * LANES)
        st = sgla_ref[b, :, pls]
        stb = st.astype(BF16)
        vbs = []
        for t in range(2):
            h = 2 * pr + t
            sl = slice(h * GLA_DV, (h + 1) * GLA_DV)
            osl = slice(RET_V + h * GLA_DV, RET_V + (h + 1) * GLA_DV)
            vb = gv_ref[b, rs, sl]
            vbs.append(vb)
            o = (jnp.dot(p["amat"][h].astype(BF16), vb, preferred_element_type=F32)
                 + lax.dot_general(p["qd"][t][:, pls], stb, _NT, preferred_element_type=F32))
            y = o * lax.rsqrt(jnp.mean(o * o, axis=-1, keepdims=True) + EPS) * gnw_ref[:, sl]
            out_ref[b, rs, osl] = (y * gate_ref[b, rs, osl]).astype(out_ref.dtype)
        kvt = lax.dot_general(jnp.concatenate(vbs, axis=0),
                              jnp.concatenate([p["kd"][0][:, pls], p["kd"][1][:, pls]], axis=0),
                              _TN, preferred_element_type=F32)
        sgla_ref[b, :, pls] = st * p["dl"][:, pls] + kvt

    fillers = []
    for c in range(n_sub):
        rs = pl.ds(c * CH, CH)
        preps = [gla_prepare(rs, b) for b in range(nb)]
        for b in range(nb):
            fillers += [functools.partial(retention, rs, b, h) for h in range(RET_HEADS)]
        for b in range(nb):
            for pr in range(GLA_HEADS // 2):
                for lv in range(len(levels)):
                    gla_level(preps[b], pr, lv)
                    if fillers and lv % 2 == 1:
                        fillers.pop(0)()
            gla_decays(preps[b])
            fillers += [functools.partial(gla_finish, rs, b, preps[b], pr) for pr in range(GLA_HEADS // 2)]
    for f in fillers:
        f()


def _mixer(ret, gate, gqk, gv, la, rnw, gnw, consts, l):
    b, lp, _ = gate.shape
    n_chunks = lp // CH
    n_sub = max(g for g in range(1, MIXER_CHUNKS_PER_STEP + 1) if n_chunks % g == 0)
    rows = n_sub * CH

    def blk(arr):
        return pl.BlockSpec((b, rows, arr.shape[2]), lambda j: (0, j, 0))

    tabs = [consts[k] for k in ("dmat", "xi", "tril", "mlev")]
    return pl.pallas_call(
        functools.partial(_mixer_kernel, consts["gch"], b, n_sub),
        grid=(n_chunks // n_sub,),
        in_specs=[blk(ret), blk(gate), blk(gqk), blk(gv), blk(la), _layer_spec(rnw, l), _layer_spec(gnw, l)]
        + [_const_spec(t.shape) for t in tabs],
        out_specs=pl.BlockSpec((b, rows, D_MIX), lambda j: (0, j, 0)),
        out_shape=jax.ShapeDtypeStruct((b, lp, D_MIX), BF16),
        scratch_shapes=[pltpu.VMEM((b, RET_HEADS, RET_DK, RET_DV), F32),
                        pltpu.VMEM((b, GLA_DV, GLA_QK), F32)],
        compiler_params=pltpu.CompilerParams(dimension_semantics=("arbitrary",),
                                             vmem_limit_bytes=VMEM_LIMIT),
        name="mixer",
    )(ret, gate, gqk, gv, la, rnw, gnw, *tabs)


def _gelu_tanh(x):
    return 0.5 * x * (1.0 + jnp.tanh(np.sqrt(2.0 / np.pi) * (x + 0.044715 * (x * x * x))))


def _ffn_part_bounds(tm):
    unit = 2 * SUBLANES
    n_units = tm // unit
    sizes = [(n_units // FFN_PARTS + (1 if t < n_units % FFN_PARTS else 0)) * unit for t in range(FFN_PARTS)]
    return [sum(sizes[:t]) for t in range(FFN_PARTS + 1)]


def _ffn_tile(tm, m_ref, h_ref, wout_ref, pmn_ref, pfn_ref, up_ref, cw_ref, cb_ref, down_ref, pon_ref,
              halo_ref, act_ref, f_ref):
    i = pl.program_id(1)

    @pl.when(i == 0)
    def _():
        halo_ref[0:8] = jnp.zeros((8, 2 * D_FF), F32)

    bounds = _ffn_part_bounds(tm)
    parts = [slice(bounds[t], bounds[t + 1]) for t in range(FFN_PARTS)]
    sub = lax.broadcasted_iota(jnp.int32, (8, 1), 0)

    m = [jnp.dot(m_ref[0, rows], wout_ref[...], preferred_element_type=F32) for rows in parts]
    h1, hn = [], []
    for t in range(FFN_PARTS):
        x = h_ref[0, parts[t]] + _rms(m[t]) * pmn_ref[...]
        h1.append(x)
        hn.append((_rms(x) * pfn_ref[...]).astype(BF16))

    def conv(t, c0):
        cs = slice(c0, c0 + FF_TILE)
        u = jnp.dot(hn[t], up_ref[:, cs], preferred_element_type=F32)
        prev = halo_ref[8 * t:8 * t + 8, cs]
        nxt = 8 * ((t + 1) % FFN_PARTS)
        halo_ref[nxt:nxt + 8, cs] = u[u.shape[0] - 8:]
        s1 = pltpu.roll(u, 1, 0)
        s2 = pltpu.roll(u, 2, 0)
        top1 = jnp.where(sub < 1, pltpu.roll(prev, 1, 0), s1[:8])
        top2 = jnp.where(sub < 2, pltpu.roll(prev, 2, 0), s2[:8])
        s1 = jnp.concatenate([top1, s1[8:]], axis=0)
        s2 = jnp.concatenate([top2, s2[8:]], axis=0)
        return (cb_ref[:, cs] + u * cw_ref[2:3, cs] + s1 * cw_ref[1:2, cs] + s2 * cw_ref[0:1, cs])

    def act_chunk(t, c0):
        a = conv(t, c0)
        g = conv(t, D_FF + c0)
        act_ref[parts[t], c0:c0 + FF_TILE] = (_gelu_tanh(a) * g).astype(BF16)

    def down_piece(t, p0):
        f_ref[parts[t], p0:p0 + 2 * LANES] = jnp.dot(act_ref[parts[t], :], down_ref[:, p0:p0 + 2 * LANES],
                                                      preferred_element_type=F32)

    chunks = list(range(0, D_FF, FF_TILE))
    for t in range(FFN_PARTS):
        pieces = list(range(0, D_MODEL, 2 * LANES)) if t > 0 else []
        for n, c0 in enumerate(chunks):
            act_chunk(t, c0)
            if pieces and n % 2 == 1:
                down_piece(t - 1, pieces.pop(0))
        for p0 in pieces:
            down_piece(t - 1, p0)
    last = FFN_PARTS - 1
    f_last = jnp.dot(act_ref[parts[last], :], down_ref[...], preferred_element_type=F32)
    h2 = jnp.concatenate([h1[t] + _rms(f_ref[parts[t], :]) * pon_ref[...] for t in range(last)]
                         + [h1[last] + _rms(f_last) * pon_ref[...]], axis=0)
    row = i * tm + lax.broadcasted_iota(jnp.int32, (tm, 1), 0)
    return jnp.where(row >= PAD, h2, 0.0)


def _ffn_kernel(tm, *refs):
    out_ref, halo_ref, act_ref, f_ref = refs[-4:]
    out_ref[0] = _ffn_tile(tm, *refs[:-4], halo_ref, act_ref, f_ref)


def _ffn_last_kernel(tm, *refs):
    out_ref, halo_ref, act_ref, f_ref, obuf_ref, osem = refs[-6:]
    b, i = pl.program_id(0), pl.program_id(1)
    nt = pl.num_programs(1)
    step = b * nt + i
    slot = step % 2

    def first_copy(s, bb):
        return pltpu.make_async_copy(obuf_ref.at[s, pl.ds(CH, tm - CH)],
                                     out_ref.at[bb, pl.ds(0, tm - CH)], osem.at[s])

    def tile_copy(s, bb, ii):
        return pltpu.make_async_copy(obuf_ref.at[s], out_ref.at[bb, pl.ds(ii * tm - CH, tm)], osem.at[s])

    def wait_step(s, bb, ii):
        @pl.when(ii == 0)
        def _():
            first_copy(s, bb).wait()

        @pl.when(ii > 0)
        def _():
            tile_copy(s, bb, ii).wait()

    res = _ffn_tile(tm, *refs[:-6], halo_ref, act_ref, f_ref)

    @pl.when(step >= 2)
    def _():
        prev = step - 2
        wait_step(slot, prev // nt, prev % nt)

    obuf_ref[slot] = res

    @pl.when(i == 0)
    def _():
        first_copy(slot, b).start()

    @pl.when(i > 0)
    def _():
        tile_copy(slot, b, i).start()

    @pl.when(step == pl.num_programs(0) * nt - 1)
    def _():
        wait_step(1 - slot, (step - 1) // nt, (step - 1) % nt)
        wait_step(slot, b, i)


def _ffn(merged, h, wout, pmn, pfn, up, cw, cb, down, pon, l, tm, last):
    b, lp, d = h.shape
    assert lp // tm >= 2
    row = lambda width: pl.BlockSpec((1, tm, width), lambda i, j: (i, j, 0))
    assert tm % (2 * SUBLANES) == 0 and tm // (2 * SUBLANES) >= FFN_PARTS
    scratch = [pltpu.VMEM((FFN_PARTS * SUBLANES, 2 * D_FF), F32), pltpu.VMEM((tm, D_FF), BF16),
               pltpu.VMEM((tm, d), F32)]
    if last:
        scratch += [pltpu.VMEM((2, tm, d), F32), pltpu.SemaphoreType.DMA((2,))]
    return pl.pallas_call(
        functools.partial(_ffn_last_kernel if last else _ffn_kernel, tm),
        grid=(b, lp // tm),
        in_specs=[row(D_MIX), row(d)] + [_const_spec(a.shape) if a.ndim == 2 else _layer_spec(a, l)
                                         for a in (wout, pmn, pfn, up, cw, cb, down, pon)],
        out_specs=pl.BlockSpec(memory_space=pl.ANY) if last else row(d),
        out_shape=jax.ShapeDtypeStruct((b, lp - CH, d) if last else (b, lp, d), F32),
        scratch_shapes=scratch,
        compiler_params=pltpu.CompilerParams(
            dimension_semantics=("arbitrary", "arbitrary") if last else ("parallel", "arbitrary"),
            vmem_limit_bytes=VMEM_LIMIT),
        name="ffn_last" if last else "ffn",
    )(merged, h, wout, pmn, pfn, up, cw, cb, down, pon)


def _rope_tables(lp):
    half = RET_DK // 2
    inv = ROPE_BASE ** (-np.arange(half, dtype=np.float64) / half)
    ang = (np.arange(lp, dtype=np.float64) - PAD)[:, None] * inv[None, :]
    c, s = np.cos(ang), np.sin(ang)
    return (jnp.asarray(np.concatenate([c, c], axis=-1), F32),
            jnp.asarray(np.concatenate([-s, s], axis=-1), F32))


def _row_tile(lp):
    for tm in (640, 512, 384, 256, 128):
        if lp % tm == 0:
            return tm
    raise ValueError(f"padded length {lp} is not a multiple of {CH}")


def kernel(x, meta_tokens, pre_mix_norm, w_in, gla_gate_w2, gla_gate_b, ret_norm_w, gla_norm_w, w_out,
           post_mix_norm, pre_ffn_norm, ffn_up, ffn_conv_w, ffn_conv_b, ffn_down, post_ffn_norm):
    bsz, seq, d = x.shape
    depth = w_in.shape[0]
    assert d == D_MODEL and seq % CH == 0
    lp = CH + seq
    tm = _row_tile(lp)

    cos2, sin2 = _rope_tables(lp)
    consts = _mixer_constants(tm)

    vec = lambda v: v.reshape(depth, 1, -1).astype(F32)
    w_b = w_in.astype(BF16)
    w_ga = w_b[:, :, MAIN_W:]
    w_ga3 = jnp.concatenate(
        [w_ga, w_ga, w_ga, jnp.zeros((depth, d, LANES - 3 * GLA_GATE_RANK), BF16)], axis=2)
    w2 = gla_gate_w2.astype(F32)
    w2_hi = w2.astype(BF16)
    w2_lo = (w2 - w2_hi.astype(F32)).astype(BF16)
    w2cat = jnp.concatenate(
        [w2_hi, w2_hi, w2_lo, jnp.zeros((depth, LANES - 3 * GLA_GATE_RANK, GLA_QK), BF16)], axis=1)
    pre_mix, gate_b, ret_nw, gla_nw = vec(pre_mix_norm), vec(gla_gate_b), vec(ret_norm_w), vec(gla_norm_w)
    ffn_w = (w_out.astype(F32), ffn_up.astype(F32), ffn_down.astype(F32))
    post_mix, pre_ffn, post_ffn = vec(post_mix_norm), vec(pre_ffn_norm), vec(post_ffn_norm)
    conv_w, conv_b = ffn_conv_w.astype(F32), vec(ffn_conv_b)
    h = x.astype(F32)
    for l in range(depth):
        outs = _proj(h, meta_tokens.astype(F32) if l == 0 else None, pre_mix, w_b, w_ga3, w2cat, gate_b,
                     cos2, sin2, consts["zeta_t"], ffn_w, l, tm)
        if l == 0:
            h = outs[-1]
        wo_b, up_b, dn_b = outs[5:8]
        merged = _mixer(*outs[:5], ret_nw, gla_nw, consts, l)
        h = _ffn(merged, h, wo_b, post_mix, pre_ffn, up_b, conv_w, conv_b, dn_b, post_ffn, l, tm,
                 last=(l == depth - 1))
    return h
```

```python
import functools

import numpy as np
import jax
import jax.numpy as jnp
from jax import lax
from jax.experimental import pallas as pl
from jax.experimental.pallas import tpu as pltpu

D_MODEL = 1024
N_META = 16
RET_HEADS = 4
RET_DK = 128
RET_DV = 128
GLA_HEADS = 4
GLA_DK = 64
GLA_DV = 128
GLA_GATE_RANK = 16
GLA_TAU = 16.0
D_FF = 2816
CONV_W = 3
ROPE_BASE = 10000.0
EPS = 1e-6

RET_QK = RET_HEADS * RET_DK
RET_V = RET_HEADS * RET_DV
GLA_QK = GLA_HEADS * GLA_DK
GLA_V = GLA_HEADS * GLA_DV
D_MIX = RET_V + GLA_V
IN_WIDTH = 2 * RET_QK + 2 * RET_V + 2 * GLA_QK + 2 * GLA_V + GLA_GATE_RANK

LANES = 128
SUBLANES = 8
CH = 128
PAD = CH - N_META
C_RQ, C_RK, C_RV, C_RG = 0, RET_QK, 2 * RET_QK, 2 * RET_QK + RET_V
C_GQ = C_RG + RET_V
C_GK, C_GV = C_GQ + GLA_QK, C_GQ + 2 * GLA_QK
C_GR = C_GV + GLA_V
MAIN_W = C_GR + GLA_V
MIXER_CHUNKS_PER_STEP = 5
FF_TILE = 256
FFN_PARTS = 2
VMEM_LIMIT = 56 * 1024 * 1024
LOG2E = float(np.log2(np.e))

BF16 = jnp.bfloat16
F32 = jnp.float32

_NT = (((1,), (1,)), ((), ()))
_TN = (((0,), (0,)), ((), ()))


def _rms(x):
    return x * lax.rsqrt(jnp.mean(x * x, axis=-1, keepdims=True) + EPS)


def _silu(g):
    return g * (1.0 / (1.0 + jnp.exp(-g)))


def _const_spec(shape):
    nd = len(shape)
    return pl.BlockSpec(shape, lambda *_: (0,) * nd, pipeline_mode=pl.Buffered(1))


def _layer_spec(arr, l):
    nd = arr.ndim - 1
    return pl.BlockSpec((None,) + arr.shape[1:], lambda *_: (l,) + (0,) * nd, pipeline_mode=pl.Buffered(1))


def _proj_tile(tm, h_ref, nw_ref, w_ref, wga_ref, w2_ref, gb_ref, cos_ref, sin_ref, zeta_ref,
               wo_f32_ref, up_f32_ref, dn_f32_ref,
               ret_ref, gate_ref, gqk_ref, gv_ref, la_ref, wo_b16_ref, up_b16_ref, dn_b16_ref):
    j = pl.program_id(1)
    hm = tm // 2
    parts = (slice(0, hm), slice(hm, tm))
    hn = {}

    def norm(rs):
        hn[rs.start] = (_rms(h_ref[0, rs]) * nw_ref[...]).astype(BF16)

    def cols(rs, c0, width):
        return jnp.dot(hn[rs.start], w_ref[:, c0:c0 + width], preferred_element_type=F32)

    def rope(rs, t, h):
        x = t[:, h * RET_DK:(h + 1) * RET_DK]
        return x * cos_ref[rs, :] + pltpu.roll(x, RET_DK // 2, 1) * sin_ref[rs, :]

    def put_q(rs, q):
        for h in range(RET_HEADS):
            ret_ref[0, rs, h * RET_DK:(h + 1) * RET_DK] = rope(rs, q, h).astype(BF16)

    def put_k(rs, k):
        for h in range(RET_HEADS):
            sl = slice(h * RET_DK, (h + 1) * RET_DK)
            kr = rope(rs, k, h)
            ret_ref[0, rs, RET_QK + h * RET_DK:RET_QK + (h + 1) * RET_DK] = kr.astype(BF16)
            ret_ref[0, rs, 2 * RET_QK + h * RET_DK:2 * RET_QK + (h + 1) * RET_DK] = (
                kr * zeta_ref[rs, sl]).astype(BF16)

    def put_v(rs, v):
        ret_ref[0, rs, 3 * RET_QK:] = v.astype(BF16)

    def put_rg(rs, g):
        gate_ref[0, rs, :RET_V] = _silu(g)

    def put_gr(rs, g):
        gate_ref[0, rs, RET_V:] = _silu(g)

    def put_gqk(rs, t):
        gqk_ref[0, rs, :GLA_QK] = t[:, :GLA_QK] * (GLA_DK ** -0.5)
        gqk_ref[0, rs, GLA_QK:] = t[:, GLA_QK:]

    def put_gv(rs, v):
        gv_ref[0, rs] = v.astype(BF16)

    def put_la(rs, g):
        g_hi = g.astype(BF16)
        g_lo = (g - g_hi.astype(F32)).astype(BF16)
        lane = lax.broadcasted_iota(jnp.int32, (1, LANES), 1)
        lo_part = (lane >= GLA_GATE_RANK) & (lane < 2 * GLA_GATE_RANK)
        z = jnp.dot(jnp.where(lo_part, g_lo, g_hi), w2_ref[...], preferred_element_type=F32) + gb_ref[...]
        la = (jnp.minimum(z, 0.0) - jnp.log(1.0 + jnp.exp(-jnp.abs(z)))) * (LOG2E / GLA_TAU)
        row = j * tm + rs.start + lax.broadcasted_iota(jnp.int32, (hm, 1), 0)
        la_ref[0, rs] = jnp.where(row >= PAD, la, 0.0)

    def cast_weights():
        for src, dst in ((wo_f32_ref, wo_b16_ref), (up_f32_ref, up_b16_ref), (dn_f32_ref, dn_b16_ref)):
            dst[...] = src[...].astype(BF16)

    def stages(rs):
        return [
            (lambda: jnp.dot(hn[rs.start], wga_ref[...], preferred_element_type=F32), put_la),
            (lambda: cols(rs, C_RQ, RET_QK), put_q),
            (lambda: cols(rs, C_RK, RET_QK), put_k),
            (lambda: cols(rs, C_RG, RET_V), put_rg),
            (lambda: cols(rs, C_GR, GLA_V), put_gr),
            (lambda: cols(rs, C_GQ, 2 * GLA_QK), put_gqk),
            (lambda: cols(rs, C_RV, RET_V), put_v),
            (lambda: cols(rs, C_GV, GLA_V), put_gv),
        ]

    per_part = [stages(rs) for rs in parts]
    order = [(rs, st) for group in zip(*per_part) for rs, st in zip(parts, group)]
    pending = None
    for rs, (matmul, epilogue) in order:
        if rs.start not in hn:
            norm(rs)
        res = matmul()
        if pending is not None:
            pending()
        pending = functools.partial(epilogue, rs, res)
    cast_weights()
    pending()


def _proj_kernel(tm, *refs):
    _proj_tile(tm, *refs)


def _proj_first_kernel(tm, x_ref, meta_ref, *refs):
    h_ref = refs[-1]
    j = pl.program_id(1)

    @pl.when(j == 0)
    def _():
        h_ref[0, :PAD] = jnp.zeros((PAD, D_MODEL), F32)
        h_ref[0, PAD:CH] = meta_ref[...]
        h_ref[0, CH:] = x_ref[:tm - CH]

    @pl.when(j > 0)
    def _():
        h_ref[0] = x_ref[...]

    _proj_tile(tm, h_ref, *refs[:-1])


def _x_window(tm, d):
    return pl.BlockSpec((None, pl.Element(tm), pl.Element(d)),
                        lambda i, j: (i, pl.multiple_of(jnp.maximum(j * tm - CH, 0), CH), 0))


def _slab_specs(arr, l, axis, unit, steps, nt):
    n_units = arr.shape[axis] // unit
    n_slabs = max(s for s in range(1, n_units + 1) if n_units % s == 0 and s <= steps)
    size = arr.shape[axis] // n_slabs
    shape = tuple(size if a == axis else arr.shape[a] for a in (1, 2))

    def pos(i, j):
        s = jnp.minimum(i * nt + j, n_slabs - 1)
        return (s, 0) if axis == 1 else (0, s)

    return (pl.BlockSpec((None,) + shape, lambda i, j: (l,) + pos(i, j)),
            pl.BlockSpec(shape, pos),
            jax.ShapeDtypeStruct(arr.shape[1:], BF16))


def _proj(h_or_x, meta, nw, w, wga, w2cat, gb, cos2, sin2, zeta_t, ffn_w, l, tm):
    first = meta is not None
    b, d = h_or_x.shape[0], h_or_x.shape[2]
    lp = h_or_x.shape[1] + (CH if first else 0)
    nt = lp // tm
    row = lambda width: pl.BlockSpec((1, tm, width), lambda i, j: (i, j, 0))
    tab = pl.BlockSpec((tm, RET_DK), lambda i, j: (j, 0))
    w_main = pl.BlockSpec((None, d, MAIN_W), lambda i, j: (l, 0, 0), pipeline_mode=pl.Buffered(1))
    slabs = [_slab_specs(ffn_w[0], l, 1, 4 * SUBLANES, b * nt, nt),
             _slab_specs(ffn_w[1], l, 2, 2 * LANES, b * nt, nt),
             _slab_specs(ffn_w[2], l, 1, 4 * SUBLANES, b * nt, nt)]
    lead = [_x_window(tm, d), _const_spec(meta.shape)] if first else [row(d)]
    in_specs = lead + [_layer_spec(nw, l), w_main, _layer_spec(wga, l), _layer_spec(w2cat, l), _layer_spec(gb, l),
                       tab, tab, _const_spec(zeta_t.shape)] + [s[0] for s in slabs]
    out_specs = ([row(4 * RET_QK), row(RET_V + GLA_V), row(2 * GLA_QK), row(GLA_V), row(GLA_QK)]
                 + [s[1] for s in slabs])
    out_shape = [jax.ShapeDtypeStruct((b, lp, 4 * RET_QK), BF16),
                 jax.ShapeDtypeStruct((b, lp, RET_V + GLA_V), F32),
                 jax.ShapeDtypeStruct((b, lp, 2 * GLA_QK), F32),
                 jax.ShapeDtypeStruct((b, lp, GLA_V), BF16),
                 jax.ShapeDtypeStruct((b, lp, GLA_QK), F32)] + [s[2] for s in slabs]
    if first:
        out_specs.append(row(d))
        out_shape.append(jax.ShapeDtypeStruct((b, lp, d), F32))
    args = ([h_or_x, meta] if first else [h_or_x]) + [nw, w, wga, w2cat, gb, cos2, sin2, zeta_t, *ffn_w]
    return pl.pallas_call(
        functools.partial(_proj_first_kernel if first else _proj_kernel, tm),
        grid=(b, nt),
        in_specs=in_specs,
        out_specs=out_specs,
        out_shape=out_shape,
        compiler_params=pltpu.CompilerParams(dimension_semantics=("arbitrary", "arbitrary"),
                                             vmem_limit_bytes=VMEM_LIMIT),
        name="proj_first" if first else "proj",
    )(*args)


def _level_halves():
    out, m = [], CH // 2
    while m >= 1:
        out.append(m)
        m //= 2
    return tuple(out)


def _mixer_constants(tm):
    c = CH
    idx = np.arange(c, dtype=np.float64)
    gam = 1.0 - 2.0 ** (-5.0 - np.arange(RET_HEADS, dtype=np.float64))
    kscale = RET_DK ** -0.5
    diff = idx[:, None] - idx[None, :]
    dmat = np.where(diff >= 0, gam[:, None, None] ** np.maximum(diff, 0.0), 0.0) * kscale
    zeta = np.repeat((gam[:, None] ** (c - 1.0 - idx)[None, :]).T, RET_DK, axis=1) * kscale
    xi = np.repeat((gam[:, None] ** (idx + 1.0)[None, :]).T, RET_DK, axis=1)
    gch = tuple(float(g ** c) for g in gam)
    s = np.arange(c)
    ml = [np.eye(c)]
    for m in _level_halves():
        blk = s // (2 * m)
        upper = (s % (2 * m)) >= m
        ml.append(((blk[:, None] == blk[None, :]) & upper[:, None] & (~upper)[None, :]).astype(np.float64))
    mlev = np.stack(ml, axis=0)
    tril = np.tril(np.ones((c, c)))
    return dict(dmat=jnp.asarray(dmat, F32), zeta_t=jnp.asarray(np.tile(zeta, (tm // c, 1)), F32),
                xi=jnp.asarray(xi, F32), gch=gch, tril=jnp.asarray(tril, BF16), mlev=jnp.asarray(mlev, F32))


def _block_ref(cum, m):
    parts = []
    for r0 in range(0, CH, 2 * m):
        parts.append(jnp.broadcast_to(cum[r0 + m - 1:r0 + m, :], (2 * m, cum.shape[1])))
    return parts[0] if len(parts) == 1 else jnp.concatenate(parts, axis=0)


def _mixer_kernel(gch, nb, n_sub,
                  ret_ref, gate_ref, gqk_ref, gv_ref, la_ref, rnw_ref, gnw_ref,
                  dmat_ref, xi_ref, tril_ref, mlev_ref,
                  out_ref, sret_ref, sgla_ref):
    j = pl.program_id(0)

    @pl.when(j == 0)
    def _():
        sret_ref[...] = jnp.zeros_like(sret_ref)
        sgla_ref[...] = jnp.zeros_like(sgla_ref)

    row = lax.broadcasted_iota(jnp.int32, (CH, 1), 0)
    lane = lax.broadcasted_iota(jnp.int32, (1, GLA_QK), 1)
    even = (lane % LANES) < GLA_DK
    tril = tril_ref[...]

    def retention(rs, b, h):
        sl = slice(h * RET_DK, (h + 1) * RET_DK)
        q = ret_ref[b, rs,sl]
        k = ret_ref[b, rs,RET_QK + h * RET_DK:RET_QK + (h + 1) * RET_DK]
        kz = ret_ref[b, rs,2 * RET_QK + h * RET_DK:2 * RET_QK + (h + 1) * RET_DK]
        vb = ret_ref[b, rs,3 * RET_QK + h * RET_DV:3 * RET_QK + (h + 1) * RET_DV]
        sc = lax.dot_general(q, k, _NT, preferred_element_type=F32) * dmat_ref[h]
        st = sret_ref[b, h]
        o = (jnp.dot(sc.astype(BF16), vb, preferred_element_type=F32)
             + jnp.dot(q, st.astype(BF16), preferred_element_type=F32) * xi_ref[:, sl])
        sret_ref[b, h] = st * gch[h] + lax.dot_general(kz, vb, _TN, preferred_element_type=F32)
        d = o - jnp.mean(o, axis=-1, keepdims=True)
        y = d * lax.rsqrt(jnp.mean(d * d, axis=-1, keepdims=True) + EPS) * rnw_ref[:, sl]
        out_ref[b, rs, sl] = (y * gate_ref[b, rs, sl]).astype(out_ref.dtype)

    def gla_prepare(rs, b):
        la = la_ref[b, rs]
        la_hi = la.astype(BF16)
        r1 = la - la_hi.astype(F32)
        la_mid = r1.astype(BF16)
        la_lo = (r1 - la_mid.astype(F32)).astype(BF16)
        cum = (jnp.dot(tril, la_hi, preferred_element_type=F32)
               + jnp.dot(tril, la_mid, preferred_element_type=F32)
               + jnp.dot(tril, la_lo, preferred_element_type=F32))
        qg = gqk_ref[b, rs,:GLA_QK]
        kg = gqk_ref[b, rs,GLA_QK:]
        return dict(la=la, cum=cum, kg=kg, qg_e=jnp.where(even, qg, 0.0), qg_o=jnp.where(even, 0.0, qg),
                    la_up=pltpu.roll(la, CH - 1, 0), la_dn=pltpu.roll(la, 1, 0),
                    amat=[None] * GLA_HEADS)

    levels = (0,) + _level_halves()

    def gla_level(p, pr, lv):
        pls = slice(pr * LANES, (pr + 1) * LANES)
        la, cum, kg, qg_e, qg_o = (p[n][:, pls] for n in ("la", "cum", "kg", "qg_e", "qg_o"))
        m = levels[lv]
        if m == 0:
            qe, qo, kh = qg_e.astype(BF16), qg_o.astype(BF16), kg.astype(BF16)
        else:
            if m >= SUBLANES // 2:
                ex = -jnp.abs(cum - _block_ref(cum, m))
            elif m == 2:
                pos = row % 4
                ex = jnp.where(pos == 0, p["la_up"][:, pls],
                               jnp.where(pos == 2, la, jnp.where(pos == 3, la + p["la_dn"][:, pls], 0.0)))
            else:
                ex = jnp.where(row % 2 == 1, la, 0.0)
            e = jnp.exp2(ex)
            qe, qo, kh = (qg_e * e).astype(BF16), (qg_o * e).astype(BF16), (kg * e).astype(BF16)
        msk = mlev_ref[lv]
        sc = lax.dot_general(jnp.concatenate([qe, qo], axis=0), kh, _NT, preferred_element_type=F32)
        for t in range(2):
            term = sc[t * CH:(t + 1) * CH] * msk
            p["amat"][2 * pr + t] = term if lv == 0 else p["amat"][2 * pr + t] + term

    def gla_decays(p):
        cum, kg = p["cum"], p["kg"]
        last = cum[CH - 1:CH, :]
        ecum = jnp.exp2(cum)
        kd = kg * jnp.exp2(last - cum)
        p.update(qd=((p["qg_e"] * ecum).astype(BF16), (p["qg_o"] * ecum).astype(BF16)),
                 kd=(jnp.where(even, kd, 0.0).astype(BF16), jnp.where(even, 0.0, kd).astype(BF16)),
                 dl=jnp.exp2(last))

    def gla_finish(rs, b, p, pr):
        pls = slice(pr * LANES, (pr + 1) * LANES)
        st = sgla_ref[b, :, pls]
        stb = st.astype(BF16)
        vbs = []
        for t in range(2):
            h = 2 * pr + t
            sl = slice(h * GLA_DV, (h + 1) * GLA_DV)
            osl = slice(RET_V + h * GLA_DV, RET_V + (h + 1) * GLA_DV)
            vb = gv_ref[b, rs, sl]
            vbs.append(vb)
            o = (jnp.dot(p["amat"][h].astype(BF16), vb, preferred_element_type=F32)
                 + lax.dot_general(p["qd"][t][:, pls], stb, _NT, preferred_element_type=F32))
            y = o * lax.rsqrt(jnp.mean(o * o, axis=-1, keepdims=True) + EPS) * gnw_ref[:, sl]
            out_ref[b, rs, osl] = (y * gate_ref[b, rs, osl]).astype(out_ref.dtype)
        kvt = lax.dot_general(jnp.concatenate(vbs, axis=0),
                              jnp.concatenate([p["kd"][0][:, pls], p["kd"][1][:, pls]], axis=0),
                              _TN, preferred_element_type=F32)
        sgla_ref[b, :, pls] = st * p["dl"][:, pls] + kvt

    fillers = []
    for c in range(n_sub):
        rs = pl.ds(c * CH, CH)
        preps = [gla_prepare(rs, b) for b in range(nb)]
        for b in range(nb):
            fillers += [functools.partial(retention, rs, b, h) for h in range(RET_HEADS)]
        for b in range(nb):
            for pr in range(GLA_HEADS // 2):
                for lv in range(len(levels)):
                    gla_level(preps[b], pr, lv)
                    if fillers and lv % 2 == 1:
                        fillers.pop(0)()
            gla_decays(preps[b])
            fillers += [functools.partial(gla_finish, rs, b, preps[b], pr) for pr in range(GLA_HEADS // 2)]
    for f in fillers:
        f()


def _mixer(ret, gate, gqk, gv, la, rnw, gnw, consts, l):
    b, lp, _ = gate.shape
    n_chunks = lp // CH
    n_sub = max(g for g in range(1, MIXER_CHUNKS_PER_STEP + 1) if n_chunks % g == 0)
    rows = n_sub * CH

    def blk(arr):
        return pl.BlockSpec((b, rows, arr.shape[2]), lambda j: (0, j, 0))

    tabs = [consts[k] for k in ("dmat", "xi", "tril", "mlev")]
    return pl.pallas_call(
        functools.partial(_mixer_kernel, consts["gch"], b, n_sub),
        grid=(n_chunks // n_sub,),
        in_specs=[blk(ret), blk(gate), blk(gqk), blk(gv), blk(la), _layer_spec(rnw, l), _layer_spec(gnw, l)]
        + [_const_spec(t.shape) for t in tabs],
        out_specs=pl.BlockSpec((b, rows, D_MIX), lambda j: (0, j, 0)),
        out_shape=jax.ShapeDtypeStruct((b, lp, D_MIX), BF16),
        scratch_shapes=[pltpu.VMEM((b, RET_HEADS, RET_DK, RET_DV), F32),
                        pltpu.VMEM((b, GLA_DV, GLA_QK), F32)],
        compiler_params=pltpu.CompilerParams(dimension_semantics=("arbitrary",),
                                             vmem_limit_bytes=VMEM_LIMIT),
        name="mixer",
    )(ret, gate, gqk, gv, la, rnw, gnw, *tabs)


def _gelu_tanh(x):
    return 0.5 * x * (1.0 + jnp.tanh(np.sqrt(2.0 / np.pi) * (x + 0.044715 * (x * x * x))))


def _ffn_part_bounds(tm):
    unit = 2 * SUBLANES
    n_units = tm // unit

    def split(n, k):
        if k == 1:
            return [n]
        for delta in range(n):
            for first in (n // k - delta, n // k + delta):
                if 0 < first <= n - (k - 1) and first % 4:
                    rest = split(n - first, k - 1)
                    if all(r % 4 for r in rest):
                        return [first] + rest
        return [n // k + (1 if t < n % k else 0) for t in range(k)]

    sizes = split(n_units, FFN_PARTS)
    return [sum(sizes[:t]) * unit for t in range(FFN_PARTS + 1)]


def _ffn_tile(tm, m_ref, h_ref, wout_ref, pmn_ref, pfn_ref, up_ref, cw_ref, cb_ref, down_ref, pon_ref,
              halo_ref, act_ref, perm_ref):
    i = pl.program_id(1)

    @pl.when(i == 0)
    def _():
        halo_ref[0:2 * SUBLANES] = jnp.zeros((2 * SUBLANES, 2 * D_FF), F32)

    bounds = _ffn_part_bounds(tm)
    parts = [slice(bounds[t], bounds[t + 1]) for t in range(FFN_PARTS)]
    strand = [(p.stop - p.start) // SUBLANES for p in parts]
    n_slabs = D_MODEL // LANES
    sub = lax.broadcasted_iota(jnp.int32, (SUBLANES, 1), 0)

    def to_strands(t, x):
        for k in range(n_slabs):
            perm_ref[k, parts[t], :] = x[:, k * LANES:(k + 1) * LANES]
        return jnp.concatenate(
            [jnp.concatenate([perm_ref[k, pl.ds(parts[t].start + v, SUBLANES, stride=strand[t]), :]
                              for k in range(n_slabs)], axis=1) for v in range(strand[t])], axis=0)

    def from_strands(t, val, p0):
        for kk in range(val.shape[1] // LANES):
            for v in range(strand[t]):
                perm_ref[p0 // LANES + kk, pl.ds(parts[t].start + v, SUBLANES, stride=strand[t]), :] = (
                    val[v * SUBLANES:(v + 1) * SUBLANES, kk * LANES:(kk + 1) * LANES])

    m = [jnp.dot(m_ref[0, rows], wout_ref[...], preferred_element_type=F32) for rows in parts]
    h1, hn = [], []
    for t in range(FFN_PARTS):
        x = h_ref[0, parts[t]] + _rms(m[t]) * pmn_ref[...]
        h1.append(x)
        hn.append(to_strands(t, _rms(x) * pfn_ref[...]).astype(BF16))

    def conv(t, c0):
        cs = slice(c0, c0 + FF_TILE)
        u = jnp.dot(hn[t], up_ref[:, cs], preferred_element_type=F32)
        rows = u.shape[0]
        prev2 = halo_ref[2 * SUBLANES * t:2 * SUBLANES * t + SUBLANES, cs]
        prev1 = halo_ref[2 * SUBLANES * t + SUBLANES:2 * SUBLANES * (t + 1), cs]
        nxt = 2 * SUBLANES * ((t + 1) % FFN_PARTS)
        halo_ref[nxt:nxt + 2 * SUBLANES, cs] = u[rows - 2 * SUBLANES:]
        top1 = jnp.where(sub < 1, pltpu.roll(prev1, 1, 0), pltpu.roll(u[rows - SUBLANES:], 1, 0))
        top2 = jnp.where(sub < 1, pltpu.roll(prev2, 1, 0),
                         pltpu.roll(u[rows - 2 * SUBLANES:rows - SUBLANES], 1, 0))
        s1 = jnp.concatenate([top1, u[:rows - SUBLANES]], axis=0)
        s2 = jnp.concatenate([top2, top1, u[:rows - 2 * SUBLANES]], axis=0)
        return (cb_ref[:, cs] + u * cw_ref[2:3, cs] + s1 * cw_ref[1:2, cs] + s2 * cw_ref[0:1, cs])

    def act_chunk(t, c0):
        a = conv(t, c0)
        g = conv(t, D_FF + c0)
        act_ref[parts[t], c0:c0 + FF_TILE] = (_gelu_tanh(a) * g).astype(BF16)

    def down_piece(t, p0):
        from_strands(t, jnp.dot(act_ref[parts[t], :], down_ref[:, p0:p0 + 2 * LANES],
                                preferred_element_type=F32), p0)

    chunks = list(range(0, D_FF, FF_TILE))
    for t in range(FFN_PARTS):
        pieces = list(range(0, D_MODEL, 2 * LANES)) if t > 0 else []
        for n, c0 in enumerate(chunks):
            act_chunk(t, c0)
            if pieces and n % 2 == 1:
                down_piece(t - 1, pieces.pop(0))
        for p0 in pieces:
            down_piece(t - 1, p0)
    for p0 in range(0, D_MODEL, 2 * LANES):
        down_piece(FFN_PARTS - 1, p0)
    h2 = []
    for t in range(FFN_PARTS):
        f = jnp.concatenate([perm_ref[k, parts[t], :] for k in range(n_slabs)], axis=1)
        h2.append(h1[t] + _rms(f) * pon_ref[...])
    h2 = jnp.concatenate(h2, axis=0)
    row = i * tm + lax.broadcasted_iota(jnp.int32, (tm, 1), 0)
    return jnp.where(row >= PAD, h2, 0.0)


def _ffn_kernel(tm, *refs):
    out_ref, halo_ref, act_ref, f_ref = refs[-4:]
    out_ref[0] = _ffn_tile(tm, *refs[:-4], halo_ref, act_ref, f_ref)


def _ffn_last_kernel(tm, *refs):
    out_ref, halo_ref, act_ref, f_ref, obuf_ref, osem = refs[-6:]
    b, i = pl.program_id(0), pl.program_id(1)
    nt = pl.num_programs(1)
    step = b * nt + i
    slot = step % 2

    def first_copy(s, bb):
        return pltpu.make_async_copy(obuf_ref.at[s, pl.ds(CH, tm - CH)],
                                     out_ref.at[bb, pl.ds(0, tm - CH)], osem.at[s])

    def tile_copy(s, bb, ii):
        return pltpu.make_async_copy(obuf_ref.at[s], out_ref.at[bb, pl.ds(ii * tm - CH, tm)], osem.at[s])

    def wait_step(s, bb, ii):
        @pl.when(ii == 0)
        def _():
            first_copy(s, bb).wait()

        @pl.when(ii > 0)
        def _():
            tile_copy(s, bb, ii).wait()

    res = _ffn_tile(tm, *refs[:-6], halo_ref, act_ref, f_ref)

    @pl.when(step >= 2)
    def _():
        prev = step - 2
        wait_step(slot, prev // nt, prev % nt)

    obuf_ref[slot] = res

    @pl.when(i == 0)
    def _():
        first_copy(slot, b).start()

    @pl.when(i > 0)
    def _():
        tile_copy(slot, b, i).start()

    @pl.when(step == pl.num_programs(0) * nt - 1)
    def _():
        wait_step(1 - slot, (step - 1) // nt, (step - 1) % nt)
        wait_step(slot, b, i)


def _ffn(merged, h, wout, pmn, pfn, up, cw, cb, down, pon, l, tm, last):
    b, lp, d = h.shape
    assert lp // tm >= 2
    row = lambda width: pl.BlockSpec((1, tm, width), lambda i, j: (i, j, 0))
    assert tm % (2 * SUBLANES) == 0 and tm // (2 * SUBLANES) >= FFN_PARTS
    scratch = [pltpu.VMEM((FFN_PARTS * 2 * SUBLANES, 2 * D_FF), F32), pltpu.VMEM((tm, D_FF), BF16),
               pltpu.VMEM((d // LANES, tm, LANES), F32)]
    if last:
        scratch += [pltpu.VMEM((2, tm, d), F32), pltpu.SemaphoreType.DMA((2,))]
    return pl.pallas_call(
        functools.partial(_ffn_last_kernel if last else _ffn_kernel, tm),
        grid=(b, lp // tm),
        in_specs=[row(D_MIX), row(d)] + [_const_spec(a.shape) if a.ndim == 2 else _layer_spec(a, l)
                                         for a in (wout, pmn, pfn, up, cw, cb, down, pon)],
        out_specs=pl.BlockSpec(memory_space=pl.ANY) if last else row(d),
        out_shape=jax.ShapeDtypeStruct((b, lp - CH, d) if last else (b, lp, d), F32),
        scratch_shapes=scratch,
        compiler_params=pltpu.CompilerParams(
            dimension_semantics=("arbitrary", "arbitrary") if last else ("parallel", "arbitrary"),
            vmem_limit_bytes=VMEM_LIMIT),
        name="ffn_last" if last else "ffn",
    )(merged, h, wout, pmn, pfn, up, cw, cb, down, pon)


def _rope_tables(lp):
    half = RET_DK // 2
    inv = ROPE_BASE ** (-np.arange(half, dtype=np.float64) / half)
    ang = (np.arange(lp, dtype=np.float64) - PAD)[:, None] * inv[None, :]
    c, s = np.cos(ang), np.sin(ang)
    return (jnp.asarray(np.concatenate([c, c], axis=-1), F32),
            jnp.asarray(np.concatenate([-s, s], axis=-1), F32))


def _row_tile(lp):
    for tm in (640, 512, 384, 256, 128):
        if lp % tm == 0:
            return tm
    raise ValueError(f"padded length {lp} is not a multiple of {CH}")


def kernel(x, meta_tokens, pre_mix_norm, w_in, gla_gate_w2, gla_gate_b, ret_norm_w, gla_norm_w, w_out,
           post_mix_norm, pre_ffn_norm, ffn_up, ffn_conv_w, ffn_conv_b, ffn_down, post_ffn_norm):
    bsz, seq, d = x.shape
    depth = w_in.shape[0]
    assert d == D_MODEL and seq % CH == 0
    lp = CH + seq
    tm = _row_tile(lp)

    cos2, sin2 = _rope_tables(lp)
    consts = _mixer_constants(tm)

    vec = lambda v: v.reshape(depth, 1, -1).astype(F32)
    w_b = w_in.astype(BF16)
    w_ga = w_b[:, :, MAIN_W:]
    w_ga3 = jnp.concatenate(
        [w_ga, w_ga, w_ga, jnp.zeros((depth, d, LANES - 3 * GLA_GATE_RANK), BF16)], axis=2)
    w2 = gla_gate_w2.astype(F32)
    w2_hi = w2.astype(BF16)
    w2_lo = (w2 - w2_hi.astype(F32)).astype(BF16)
    w2cat = jnp.concatenate(
        [w2_hi, w2_hi, w2_lo, jnp.zeros((depth, LANES - 3 * GLA_GATE_RANK, GLA_QK), BF16)], axis=1)
    pre_mix, gate_b, ret_nw, gla_nw = vec(pre_mix_norm), vec(gla_gate_b), vec(ret_norm_w), vec(gla_norm_w)
    ffn_w = (w_out.astype(F32), ffn_up.astype(F32), ffn_down.astype(F32))
    post_mix, pre_ffn, post_ffn = vec(post_mix_norm), vec(pre_ffn_norm), vec(post_ffn_norm)
    conv_w, conv_b = ffn_conv_w.astype(F32), vec(ffn_conv_b)
    h = x.astype(F32)
    for l in range(depth):
        outs = _proj(h, meta_tokens.astype(F32) if l == 0 else None, pre_mix, w_b, w_ga3, w2cat, gate_b,
                     cos2, sin2, consts["zeta_t"], ffn_w, l, tm)
        if l == 0:
            h = outs[-1]
        wo_b, up_b, dn_b = outs[5:8]
        merged = _mixer(*outs[:5], ret_nw, gla_nw, consts, l)
        h = _ffn(merged, h, wo_b, post_mix, pre_ffn, up_b, conv_w, conv_b, dn_b, post_ffn, l, tm,
                 last=(l == depth - 1))
    return h
```

```python
import functools

import numpy as np
import jax
import jax.numpy as jnp
from jax import lax
from jax.experimental import pallas as pl
from jax.experimental.pallas import tpu as pltpu

D_MODEL = 1024
N_META = 16
RET_HEADS = 4
RET_DK = 128
RET_DV = 128
GLA_HEADS = 4
GLA_DK = 64
GLA_DV = 128
GLA_GATE_RANK = 16
GLA_TAU = 16.0
D_FF = 2816
CONV_W = 3
ROPE_BASE = 10000.0
EPS = 1e-6

RET_QK = RET_HEADS * RET_DK
RET_V = RET_HEADS * RET_DV
GLA_QK = GLA_HEADS * GLA_DK
GLA_V = GLA_HEADS * GLA_DV
D_MIX = RET_V + GLA_V
IN_WIDTH = 2 * RET_QK + 2 * RET_V + 2 * GLA_QK + 2 * GLA_V + GLA_GATE_RANK

LANES = 128
SUBLANES = 8
CH = 128
SUB_ROWS = CH // 2
PAD = CH - N_META
C_RQ, C_RK, C_RV, C_RG = 0, RET_QK, 2 * RET_QK, 2 * RET_QK + RET_V
C_GQ = C_RG + RET_V
C_GK, C_GV = C_GQ + GLA_QK, C_GQ + 2 * GLA_QK
C_GR = C_GV + GLA_V
MAIN_W = C_GR + GLA_V
MIXER_CHUNKS_PER_STEP = 5
FF_TILE = 256
FFN_PARTS = 2
VMEM_LIMIT = 56 * 1024 * 1024
LOG2E = float(np.log2(np.e))

BF16 = jnp.bfloat16
F32 = jnp.float32

_NT = (((1,), (1,)), ((), ()))
_TN = (((0,), (0,)), ((), ()))


def _rms(x):
    return x * lax.rsqrt(jnp.mean(x * x, axis=-1, keepdims=True) + EPS)


def _silu(g):
    return g * (1.0 / (1.0 + jnp.exp(-g)))


def _const_spec(shape):
    nd = len(shape)
    return pl.BlockSpec(shape, lambda *_: (0,) * nd, pipeline_mode=pl.Buffered(1))


def _layer_spec(arr, l):
    nd = arr.ndim - 1
    return pl.BlockSpec((None,) + arr.shape[1:], lambda *_: (l,) + (0,) * nd, pipeline_mode=pl.Buffered(1))


def _proj_tile(tm, h_ref, nw_ref, w_ref, wga_ref, w2_ref, gb_ref, cos_ref, sin_ref, zeta_ref,
               wo_f32_ref, up_f32_ref, dn_f32_ref,
               ret_ref, gate_ref, gqk_ref, gv_ref, la_ref, wo_b16_ref, up_b16_ref, dn_b16_ref):
    j = pl.program_id(1)
    hm = tm // 2
    parts = (slice(0, hm), slice(hm, tm))
    hn = {}

    def norm(rs):
        hn[rs.start] = (_rms(h_ref[0, rs]) * nw_ref[...]).astype(BF16)

    def cols(rs, c0, width):
        return jnp.dot(hn[rs.start], w_ref[:, c0:c0 + width], preferred_element_type=F32)

    def rope(rs, t, h):
        x = t[:, h * RET_DK:(h + 1) * RET_DK]
        return x * cos_ref[rs, :] + pltpu.roll(x, RET_DK // 2, 1) * sin_ref[rs, :]

    def put_q(rs, q):
        for h in range(RET_HEADS):
            ret_ref[0, rs, h * RET_DK:(h + 1) * RET_DK] = rope(rs, q, h).astype(BF16)

    def put_k(rs, k):
        for h in range(RET_HEADS):
            sl = slice(h * RET_DK, (h + 1) * RET_DK)
            kr = rope(rs, k, h)
            ret_ref[0, rs, RET_QK + h * RET_DK:RET_QK + (h + 1) * RET_DK] = kr.astype(BF16)
            ret_ref[0, rs, 2 * RET_QK + h * RET_DK:2 * RET_QK + (h + 1) * RET_DK] = (
                kr * zeta_ref[rs, sl]).astype(BF16)

    def put_v(rs, v):
        ret_ref[0, rs, 3 * RET_QK:] = v.astype(BF16)

    def put_rg(rs, g):
        gate_ref[0, rs, :RET_V] = _silu(g)

    def put_gr(rs, g):
        gate_ref[0, rs, RET_V:] = _silu(g)

    def put_gqk(rs, t):
        gqk_ref[0, rs, :GLA_QK] = t[:, :GLA_QK] * (GLA_DK ** -0.5)
        gqk_ref[0, rs, GLA_QK:] = t[:, GLA_QK:]

    def put_gv(rs, v):
        gv_ref[0, rs] = v.astype(BF16)

    def put_la(rs, g):
        g_hi = g.astype(BF16)
        g_lo = (g - g_hi.astype(F32)).astype(BF16)
        lane = lax.broadcasted_iota(jnp.int32, (1, LANES), 1)
        lo_part = (lane >= GLA_GATE_RANK) & (lane < 2 * GLA_GATE_RANK)
        z = jnp.dot(jnp.where(lo_part, g_lo, g_hi), w2_ref[...], preferred_element_type=F32) + gb_ref[...]
        la = (jnp.minimum(z, 0.0) - jnp.log(1.0 + jnp.exp(-jnp.abs(z)))) * (LOG2E / GLA_TAU)
        row = j * tm + rs.start + lax.broadcasted_iota(jnp.int32, (hm, 1), 0)
        la_ref[0, rs] = jnp.where(row >= PAD, la, 0.0)

    def cast_weights():
        for src, dst in ((wo_f32_ref, wo_b16_ref), (up_f32_ref, up_b16_ref), (dn_f32_ref, dn_b16_ref)):
            dst[...] = src[...].astype(BF16)

    def stages(rs):
        return [
            (lambda: jnp.dot(hn[rs.start], wga_ref[...], preferred_element_type=F32), put_la),
            (lambda: cols(rs, C_RQ, RET_QK), put_q),
            (lambda: cols(rs, C_RK, RET_QK), put_k),
            (lambda: cols(rs, C_RG, RET_V), put_rg),
            (lambda: cols(rs, C_GR, GLA_V), put_gr),
            (lambda: cols(rs, C_GQ, 2 * GLA_QK), put_gqk),
            (lambda: cols(rs, C_RV, RET_V), put_v),
            (lambda: cols(rs, C_GV, GLA_V), put_gv),
        ]

    per_part = [stages(rs) for rs in parts]
    order = [(rs, st) for group in zip(*per_part) for rs, st in zip(parts, group)]
    pending = None
    for rs, (matmul, epilogue) in order:
        if rs.start not in hn:
            norm(rs)
        res = matmul()
        if pending is not None:
            pending()
        pending = functools.partial(epilogue, rs, res)
    cast_weights()
    pending()


def _proj_kernel(tm, *refs):
    _proj_tile(tm, *refs)


def _proj_first_kernel(tm, x_ref, meta_ref, *refs):
    h_ref = refs[-1]
    j = pl.program_id(1)

    @pl.when(j == 0)
    def _():
        h_ref[0, :PAD] = jnp.zeros((PAD, D_MODEL), F32)
        h_ref[0, PAD:CH] = meta_ref[...]
        h_ref[0, CH:] = x_ref[:tm - CH]

    @pl.when(j > 0)
    def _():
        h_ref[0] = x_ref[...]

    _proj_tile(tm, h_ref, *refs[:-1])


def _x_window(tm, d):
    return pl.BlockSpec((None, pl.Element(tm), pl.Element(d)),
                        lambda i, j: (i, pl.multiple_of(jnp.maximum(j * tm - CH, 0), CH), 0))


def _slab_specs(arr, l, axis, unit, steps, nt):
    n_units = arr.shape[axis] // unit
    n_slabs = max(s for s in range(1, n_units + 1) if n_units % s == 0 and s <= steps)
    size = arr.shape[axis] // n_slabs
    shape = tuple(size if a == axis else arr.shape[a] for a in (1, 2))

    def pos(i, j):
        s = jnp.minimum(i * nt + j, n_slabs - 1)
        return (s, 0) if axis == 1 else (0, s)

    return (pl.BlockSpec((None,) + shape, lambda i, j: (l,) + pos(i, j)),
            pl.BlockSpec(shape, pos),
            jax.ShapeDtypeStruct(arr.shape[1:], BF16))


def _proj(h_or_x, meta, nw, w, wga, w2cat, gb, cos2, sin2, zeta_t, ffn_w, l, tm):
    first = meta is not None
    b, d = h_or_x.shape[0], h_or_x.shape[2]
    lp = h_or_x.shape[1] + (CH if first else 0)
    nt = lp // tm
    row = lambda width: pl.BlockSpec((1, tm, width), lambda i, j: (i, j, 0))
    tab = pl.BlockSpec((tm, RET_DK), lambda i, j: (j, 0))
    w_main = pl.BlockSpec((None, d, MAIN_W), lambda i, j: (l, 0, 0), pipeline_mode=pl.Buffered(1))
    slabs = [_slab_specs(ffn_w[0], l, 1, 4 * SUBLANES, b * nt, nt),
             _slab_specs(ffn_w[1], l, 2, 2 * LANES, b * nt, nt),
             _slab_specs(ffn_w[2], l, 1, 4 * SUBLANES, b * nt, nt)]
    lead = [_x_window(tm, d), _const_spec(meta.shape)] if first else [row(d)]
    in_specs = lead + [_layer_spec(nw, l), w_main, _layer_spec(wga, l), _layer_spec(w2cat, l), _layer_spec(gb, l),
                       tab, tab, _const_spec(zeta_t.shape)] + [s[0] for s in slabs]
    out_specs = ([row(4 * RET_QK), row(RET_V + GLA_V), row(2 * GLA_QK), row(GLA_V), row(GLA_QK)]
                 + [s[1] for s in slabs])
    out_shape = [jax.ShapeDtypeStruct((b, lp, 4 * RET_QK), BF16),
                 jax.ShapeDtypeStruct((b, lp, RET_V + GLA_V), F32),
                 jax.ShapeDtypeStruct((b, lp, 2 * GLA_QK), F32),
                 jax.ShapeDtypeStruct((b, lp, GLA_V), BF16),
                 jax.ShapeDtypeStruct((b, lp, GLA_QK), F32)] + [s[2] for s in slabs]
    if first:
        out_specs.append(row(d))
        out_shape.append(jax.ShapeDtypeStruct((b, lp, d), F32))
    args = ([h_or_x, meta] if first else [h_or_x]) + [nw, w, wga, w2cat, gb, cos2, sin2, zeta_t, *ffn_w]
    return pl.pallas_call(
        functools.partial(_proj_first_kernel if first else _proj_kernel, tm),
        grid=(b, nt),
        in_specs=in_specs,
        out_specs=out_specs,
        out_shape=out_shape,
        compiler_params=pltpu.CompilerParams(dimension_semantics=("arbitrary", "arbitrary"),
                                             vmem_limit_bytes=VMEM_LIMIT),
        name="proj_first" if first else "proj",
    )(*args)


def _level_halves():
    out, m = [], SUB_ROWS // 2
    while m >= 1:
        out.append(m)
        m //= 2
    return tuple(out)


def _mixer_constants(tm):
    c = CH
    idx = np.arange(c, dtype=np.float64)
    gam = 1.0 - 2.0 ** (-5.0 - np.arange(RET_HEADS, dtype=np.float64))
    kscale = RET_DK ** -0.5
    diff = idx[:, None] - idx[None, :]
    dmat = np.where(diff >= 0, gam[:, None, None] ** np.maximum(diff, 0.0), 0.0) * kscale
    zeta = np.repeat((gam[:, None] ** (c - 1.0 - idx)[None, :]).T, RET_DK, axis=1) * kscale
    xi = np.repeat((gam[:, None] ** (idx + 1.0)[None, :]).T, RET_DK, axis=1)
    gch = tuple(float(g ** c) for g in gam)
    s = np.arange(SUB_ROWS)
    t = np.arange(2 * SUB_ROWS) % SUB_ROWS
    ml = [(s[:, None] == t[None, :]).astype(np.float64)]
    for m in _level_halves():
        blk_s, blk_t = s // (2 * m), t // (2 * m)
        up_s, up_t = (s % (2 * m)) >= m, (t % (2 * m)) >= m
        ml.append(((blk_s[:, None] == blk_t[None, :]) & up_s[:, None] & (~up_t)[None, :]).astype(np.float64))
    mlev = np.stack(ml, axis=0)
    tril = np.tril(np.ones((c, c)))
    return dict(dmat=jnp.asarray(dmat, F32), zeta_t=jnp.asarray(np.tile(zeta, (tm // c, 1)), F32),
                xi=jnp.asarray(xi, F32), gch=gch, tril=jnp.asarray(tril, BF16), mlev=jnp.asarray(mlev, F32))


def _block_ref(cum, m):
    parts = []
    for r0 in range(0, cum.shape[0], 2 * m):
        parts.append(jnp.broadcast_to(cum[r0 + m - 1:r0 + m, :], (2 * m, cum.shape[1])))
    return parts[0] if len(parts) == 1 else jnp.concatenate(parts, axis=0)


def _mixer_kernel(gch, nb, n_sub,
                  ret_ref, gate_ref, gqk_ref, gv_ref, la_ref, rnw_ref, gnw_ref,
                  dmat_ref, xi_ref, tril_ref, mlev_ref,
                  out_ref, sret_ref, sgla_ref):
    j = pl.program_id(0)

    @pl.when(j == 0)
    def _():
        sret_ref[...] = jnp.zeros_like(sret_ref)
        sgla_ref[...] = jnp.zeros_like(sgla_ref)

    row = lax.broadcasted_iota(jnp.int32, (CH, 1), 0)
    lane = lax.broadcasted_iota(jnp.int32, (1, GLA_QK), 1)
    even = (lane % LANES) < GLA_DK
    tril = tril_ref[...]

    def retention(rs, b, h):
        sl = slice(h * RET_DK, (h + 1) * RET_DK)
        q = ret_ref[b, rs,sl]
        k = ret_ref[b, rs,RET_QK + h * RET_DK:RET_QK + (h + 1) * RET_DK]
        kz = ret_ref[b, rs,2 * RET_QK + h * RET_DK:2 * RET_QK + (h + 1) * RET_DK]
        vb = ret_ref[b, rs,3 * RET_QK + h * RET_DV:3 * RET_QK + (h + 1) * RET_DV]
        sc = lax.dot_general(q, k, _NT, preferred_element_type=F32) * dmat_ref[h]
        st = sret_ref[b, h]
        o = (jnp.dot(sc.astype(BF16), vb, preferred_element_type=F32)
             + jnp.dot(q, st.astype(BF16), preferred_element_type=F32) * xi_ref[:, sl])
        sret_ref[b, h] = st * gch[h] + lax.dot_general(kz, vb, _TN, preferred_element_type=F32)
        d = o - jnp.mean(o, axis=-1, keepdims=True)
        y = d * lax.rsqrt(jnp.mean(d * d, axis=-1, keepdims=True) + EPS) * rnw_ref[:, sl]
        out_ref[b, rs, sl] = (y * gate_ref[b, rs, sl]).astype(out_ref.dtype)

    def gla_prepare(rs, b):
        la = la_ref[b, rs]
        la_hi = la.astype(BF16)
        r1 = la - la_hi.astype(F32)
        la_mid = r1.astype(BF16)
        la_lo = (r1 - la_mid.astype(F32)).astype(BF16)
        cum = (jnp.dot(tril, la_hi, preferred_element_type=F32)
               + jnp.dot(tril, la_mid, preferred_element_type=F32)
               + jnp.dot(tril, la_lo, preferred_element_type=F32))
        qg = gqk_ref[b, rs,:GLA_QK]
        kg = gqk_ref[b, rs,GLA_QK:]
        return dict(la=la, cum=cum, qg=qg, kg_e=jnp.where(even, kg, 0.0), kg_o=jnp.where(even, 0.0, kg),
                    la_up=pltpu.roll(la, CH - 1, 0), la_dn=pltpu.roll(la, 1, 0),
                    acc={}, top={})

    levels = (0,) + _level_halves()
    subs = (slice(0, SUB_ROWS), slice(SUB_ROWS, CH))

    def head_keys(p, rows, pls, e):
        ke, ko = p["kg_e"][rows, pls], p["kg_o"][rows, pls]
        if e is not None:
            ke, ko = ke * e, ko * e
        return jnp.concatenate([ke.astype(BF16), ko.astype(BF16)], axis=0)

    def gla_level(p, pr, lv):
        pls = slice(pr * LANES, (pr + 1) * LANES)
        la, cum, qg = (p[n][:, pls] for n in ("la", "cum", "qg"))
        m = levels[lv]
        if m == 0:
            e = None
            qh = qg.astype(BF16)
        else:
            if m >= SUBLANES // 2:
                ex = -jnp.abs(cum - _block_ref(cum, m))
            elif m == 2:
                pos = row % 4
                ex = jnp.where(pos == 0, p["la_up"][:, pls],
                               jnp.where(pos == 2, la, jnp.where(pos == 3, la + p["la_dn"][:, pls], 0.0)))
            else:
                ex = jnp.where(row % 2 == 1, la, 0.0)
            e = jnp.exp2(ex)
            qh = (qg * e).astype(BF16)
        keys = jnp.concatenate([head_keys(p, rows, pls, None if e is None else e[rows]) for rows in subs], axis=0)
        sc = lax.dot_general(qh, keys, _NT, preferred_element_type=F32)
        for u, rows in enumerate(subs):
            term = sc[rows, u * 2 * SUB_ROWS:(u + 1) * 2 * SUB_ROWS] * mlev_ref[lv]
            p["acc"][pr, u] = term if lv == 0 else p["acc"][pr, u] + term

    def gla_cross(p, pr):
        pls = slice(pr * LANES, (pr + 1) * LANES)
        cum = p["cum"][:, pls]
        cref = cum[SUB_ROWS - 1:SUB_ROWS]
        qh = (p["qg"][subs[1], pls] * jnp.exp2(cum[subs[1]] - cref)).astype(BF16)
        p["top"][pr] = lax.dot_general(qh, head_keys(p, subs[0], pls, jnp.exp2(cref - cum[subs[0]])), _NT,
                                       preferred_element_type=F32)

    def gla_decays(p):
        cum = p["cum"]
        last = cum[CH - 1:CH, :]
        qd = p["qg"] * jnp.exp2(cum)
        dk = jnp.exp2(last - cum)
        p.update(qd=(jnp.where(even, qd, 0.0).astype(BF16), jnp.where(even, 0.0, qd).astype(BF16)),
                 kd=((p["kg_e"] * dk).astype(BF16), (p["kg_o"] * dk).astype(BF16)),
                 dl=jnp.exp2(last))

    def gla_finish(rs, b, p, pr):
        pls = slice(pr * LANES, (pr + 1) * LANES)
        st = sgla_ref[b, :, pls]
        stb = st.astype(BF16)
        vbs = [gv_ref[b, rs, (2 * pr + t) * GLA_DV:(2 * pr + t + 1) * GLA_DV] for t in range(2)]
        vst = [jnp.concatenate([vbs[0][rows], vbs[1][rows]], axis=0) for rows in subs]
        vst_cross = jnp.concatenate([vst[1], vst[0]], axis=0)
        for t in range(2):
            h = 2 * pr + t
            sl = slice(h * GLA_DV, (h + 1) * GLA_DV)
            osl = slice(RET_V + h * GLA_DV, RET_V + (h + 1) * GLA_DV)
            mine = even[:, :LANES] if t == 0 else ~even[:, :LANES]
            a0, a1, ax = (jnp.where(mine, a, 0.0).astype(BF16)
                          for a in (p["acc"][pr, 0], p["acc"][pr, 1], p["top"][pr]))
            o = (jnp.concatenate([jnp.dot(a0, vst[0], preferred_element_type=F32),
                                  jnp.dot(jnp.concatenate([a1, ax], axis=1), vst_cross,
                                          preferred_element_type=F32)], axis=0)
                 + lax.dot_general(p["qd"][t][:, pls], stb, _NT, preferred_element_type=F32))
            y = o * lax.rsqrt(jnp.mean(o * o, axis=-1, keepdims=True) + EPS) * gnw_ref[:, sl]
            out_ref[b, rs, osl] = (y * gate_ref[b, rs, osl]).astype(out_ref.dtype)
        kvt = lax.dot_general(jnp.concatenate(vbs, axis=0),
                              jnp.concatenate([p["kd"][0][:, pls], p["kd"][1][:, pls]], axis=0),
                              _TN, preferred_element_type=F32)
        sgla_ref[b, :, pls] = st * p["dl"][:, pls] + kvt

    fillers = []
    for c in range(n_sub):
        rs = pl.ds(c * CH, CH)
        preps = [gla_prepare(rs, b) for b in range(nb)]
        for b in range(nb):
            fillers += [functools.partial(retention, rs, b, h) for h in range(RET_HEADS)]
        for b in range(nb):
            for pr in range(GLA_HEADS // 2):
                gla_cross(preps[b], pr)
                for lv in range(len(levels)):
                    gla_level(preps[b], pr, lv)
                    if fillers:
                        fillers.pop(0)()
            gla_decays(preps[b])
            fillers += [functools.partial(gla_finish, rs, b, preps[b], pr) for pr in range(GLA_HEADS // 2)]
    for f in fillers:
        f()


def _mixer(ret, gate, gqk, gv, la, rnw, gnw, consts, l):
    b, lp, _ = gate.shape
    n_chunks = lp // CH
    n_sub = max(g for g in range(1, MIXER_CHUNKS_PER_STEP + 1) if n_chunks % g == 0)
    rows = n_sub * CH

    def blk(arr):
        return pl.BlockSpec((b, rows, arr.shape[2]), lambda j: (0, j, 0))

    tabs = [consts[k] for k in ("dmat", "xi", "tril", "mlev")]
    return pl.pallas_call(
        functools.partial(_mixer_kernel, consts["gch"], b, n_sub),
        grid=(n_chunks // n_sub,),
        in_specs=[blk(ret), blk(gate), blk(gqk), blk(gv), blk(la), _layer_spec(rnw, l), _layer_spec(gnw, l)]
        + [_const_spec(t.shape) for t in tabs],
        out_specs=pl.BlockSpec((b, rows, D_MIX), lambda j: (0, j, 0)),
        out_shape=jax.ShapeDtypeStruct((b, lp, D_MIX), BF16),
        scratch_shapes=[pltpu.VMEM((b, RET_HEADS, RET_DK, RET_DV), F32),
                        pltpu.VMEM((b, GLA_DV, GLA_QK), F32)],
        compiler_params=pltpu.CompilerParams(dimension_semantics=("arbitrary",),
                                             vmem_limit_bytes=VMEM_LIMIT),
        name="mixer",
    )(ret, gate, gqk, gv, la, rnw, gnw, *tabs)


def _gelu_tanh(x):
    return 0.5 * x * (1.0 + jnp.tanh(np.sqrt(2.0 / np.pi) * (x + 0.044715 * (x * x * x))))


def _ffn_part_bounds(tm):
    unit = 2 * SUBLANES
    n_units = tm // unit

    def split(n, k):
        if k == 1:
            return [n]
        for delta in range(n):
            for first in (n // k - delta, n // k + delta):
                if 0 < first <= n - (k - 1) and first % 4:
                    rest = split(n - first, k - 1)
                    if all(r % 4 for r in rest):
                        return [first] + rest
        return [n // k + (1 if t < n % k else 0) for t in range(k)]

    sizes = split(n_units, FFN_PARTS)
    return [sum(sizes[:t]) * unit for t in range(FFN_PARTS + 1)]


def _ffn_tile(tm, m_ref, h_ref, wout_ref, pmn_ref, pfn_ref, up_ref, cw_ref, cb_ref, down_ref, pon_ref,
              halo_ref, act_ref, perm_ref):
    i = pl.program_id(1)

    @pl.when(i == 0)
    def _():
        halo_ref[0:2 * SUBLANES] = jnp.zeros((2 * SUBLANES, 2 * D_FF), F32)

    bounds = _ffn_part_bounds(tm)
    parts = [slice(bounds[t], bounds[t + 1]) for t in range(FFN_PARTS)]
    strand = [(p.stop - p.start) // SUBLANES for p in parts]
    n_slabs = D_MODEL // LANES
    sub = lax.broadcasted_iota(jnp.int32, (SUBLANES, 1), 0)

    def to_strands(t, x):
        for k in range(n_slabs):
            perm_ref[k, parts[t], :] = x[:, k * LANES:(k + 1) * LANES]
        return jnp.concatenate(
            [jnp.concatenate([perm_ref[k, pl.ds(parts[t].start + v, SUBLANES, stride=strand[t]), :]
                              for k in range(n_slabs)], axis=1) for v in range(strand[t])], axis=0)

    def from_strands(t, val, p0):
        for kk in range(val.shape[1] // LANES):
            for v in range(strand[t]):
                perm_ref[p0 // LANES + kk, pl.ds(parts[t].start + v, SUBLANES, stride=strand[t]), :] = (
                    val[v * SUBLANES:(v + 1) * SUBLANES, kk * LANES:(kk + 1) * LANES])

    m = [jnp.dot(m_ref[0, rows], wout_ref[...], preferred_element_type=F32) for rows in parts]
    h1, hn = [], []
    for t in range(FFN_PARTS):
        x = h_ref[0, parts[t]] + _rms(m[t]) * pmn_ref[...]
        h1.append(x)
        hn.append(to_strands(t, _rms(x) * pfn_ref[...]).astype(BF16))

    def conv(t, c0):
        cs = slice(c0, c0 + FF_TILE)
        u = jnp.dot(hn[t], up_ref[:, cs], preferred_element_type=F32)
        rows = u.shape[0]
        prev2 = halo_ref[2 * SUBLANES * t:2 * SUBLANES * t + SUBLANES, cs]
        prev1 = halo_ref[2 * SUBLANES * t + SUBLANES:2 * SUBLANES * (t + 1), cs]
        nxt = 2 * SUBLANES * ((t + 1) % FFN_PARTS)
        halo_ref[nxt:nxt + 2 * SUBLANES, cs] = u[rows - 2 * SUBLANES:]
        top1 = jnp.where(sub < 1, pltpu.roll(prev1, 1, 0), pltpu.roll(u[rows - SUBLANES:], 1, 0))
        top2 = jnp.where(sub < 1, pltpu.roll(prev2, 1, 0),
                         pltpu.roll(u[rows - 2 * SUBLANES:rows - SUBLANES], 1, 0))
        s1 = jnp.concatenate([top1, u[:rows - SUBLANES]], axis=0)
        s2 = jnp.concatenate([top2, top1, u[:rows - 2 * SUBLANES]], axis=0)
        return (cb_ref[:, cs] + u * cw_ref[2:3, cs] + s1 * cw_ref[1:2, cs] + s2 * cw_ref[0:1, cs])

    def act_chunk(t, c0):
        a = conv(t, c0)
        g = conv(t, D_FF + c0)
        act_ref[parts[t], c0:c0 + FF_TILE] = (_gelu_tanh(a) * g).astype(BF16)

    def down_piece(t, p0):
        from_strands(t, jnp.dot(act_ref[parts[t], :], down_ref[:, p0:p0 + 2 * LANES],
                                preferred_element_type=F32), p0)

    chunks = list(range(0, D_FF, FF_TILE))
    for t in range(FFN_PARTS):
        pieces = list(range(0, D_MODEL, 2 * LANES)) if t > 0 else []
        for n, c0 in enumerate(chunks):
            act_chunk(t, c0)
            if pieces and n % 2 == 1:
                down_piece(t - 1, pieces.pop(0))
        for p0 in pieces:
            down_piece(t - 1, p0)
    for p0 in range(0, D_MODEL, 2 * LANES):
        down_piece(FFN_PARTS - 1, p0)
    h2 = []
    for t in range(FFN_PARTS):
        f = jnp.concatenate([perm_ref[k, parts[t], :] for k in range(n_slabs)], axis=1)
        h2.append(h1[t] + _rms(f) * pon_ref[...])
    h2 = jnp.concatenate(h2, axis=0)
    row = i * tm + lax.broadcasted_iota(jnp.int32, (tm, 1), 0)
    return jnp.where(row >= PAD, h2, 0.0)


def _ffn_kernel(tm, *refs):
    out_ref, halo_ref, act_ref, f_ref = refs[-4:]
    out_ref[0] = _ffn_tile(tm, *refs[:-4], halo_ref, act_ref, f_ref)


def _ffn_last_kernel(tm, *refs):
    out_ref, halo_ref, act_ref, f_ref, obuf_ref, osem = refs[-6:]
    b, i = pl.program_id(0), pl.program_id(1)
    nt = pl.num_programs(1)
    step = b * nt + i
    slot = step % 2

    def first_copy(s, bb):
        return pltpu.make_async_copy(obuf_ref.at[s, pl.ds(CH, tm - CH)],
                                     out_ref.at[bb, pl.ds(0, tm - CH)], osem.at[s])

    def tile_copy(s, bb, ii):
        return pltpu.make_async_copy(obuf_ref.at[s], out_ref.at[bb, pl.ds(ii * tm - CH, tm)], osem.at[s])

    def wait_step(s, bb, ii):
        @pl.when(ii == 0)
        def _():
            first_copy(s, bb).wait()

        @pl.when(ii > 0)
        def _():
            tile_copy(s, bb, ii).wait()

    res = _ffn_tile(tm, *refs[:-6], halo_ref, act_ref, f_ref)

    @pl.when(step >= 2)
    def _():
        prev = step - 2
        wait_step(slot, prev // nt, prev % nt)

    obuf_ref[slot] = res

    @pl.when(i == 0)
    def _():
        first_copy(slot, b).start()

    @pl.when(i > 0)
    def _():
        tile_copy(slot, b, i).start()

    @pl.when(step == pl.num_programs(0) * nt - 1)
    def _():
        wait_step(1 - slot, (step - 1) // nt, (step - 1) % nt)
        wait_step(slot, b, i)


def _ffn(merged, h, wout, pmn, pfn, up, cw, cb, down, pon, l, tm, last):
    b, lp, d = h.shape
    assert lp // tm >= 2
    row = lambda width: pl.BlockSpec((1, tm, width), lambda i, j: (i, j, 0))
    assert tm % (2 * SUBLANES) == 0 and tm // (2 * SUBLANES) >= FFN_PARTS
    scratch = [pltpu.VMEM((FFN_PARTS * 2 * SUBLANES, 2 * D_FF), F32), pltpu.VMEM((tm, D_FF), BF16),
               pltpu.VMEM((d // LANES, tm, LANES), F32)]
    if last:
        scratch += [pltpu.VMEM((2, tm, d), F32), pltpu.SemaphoreType.DMA((2,))]
    return pl.pallas_call(
        functools.partial(_ffn_last_kernel if last else _ffn_kernel, tm),
        grid=(b, lp // tm),
        in_specs=[row(D_MIX), row(d)] + [_const_spec(a.shape) if a.ndim == 2 else _layer_spec(a, l)
                                         for a in (wout, pmn, pfn, up, cw, cb, down, pon)],
        out_specs=pl.BlockSpec(memory_space=pl.ANY) if last else row(d),
        out_shape=jax.ShapeDtypeStruct((b, lp - CH, d) if last else (b, lp, d), F32),
        scratch_shapes=scratch,
        compiler_params=pltpu.CompilerParams(
            dimension_semantics=("arbitrary", "arbitrary") if last else ("parallel", "arbitrary"),
            vmem_limit_bytes=VMEM_LIMIT),
        name="ffn_last" if last else "ffn",
    )(merged, h, wout, pmn, pfn, up, cw, cb, down, pon)


def _rope_tables(lp):
    half = RET_DK // 2
    inv = ROPE_BASE ** (-np.arange(half, dtype=np.float64) / half)
    ang = (np.arange(lp, dtype=np.float64) - PAD)[:, None] * inv[None, :]
    c, s = np.cos(ang), np.sin(ang)
    return (jnp.asarray(np.concatenate([c, c], axis=-1), F32),
            jnp.asarray(np.concatenate([-s, s], axis=-1), F32))


def _row_tile(lp):
    for tm in (640, 512, 384, 256, 128):
        if lp % tm == 0:
            return tm
    raise ValueError(f"padded length {lp} is not a multiple of {CH}")


def kernel(x, meta_tokens, pre_mix_norm, w_in, gla_gate_w2, gla_gate_b, ret_norm_w, gla_norm_w, w_out,
           post_mix_norm, pre_ffn_norm, ffn_up, ffn_conv_w, ffn_conv_b, ffn_down, post_ffn_norm):
    bsz, seq, d = x.shape
    depth = w_in.shape[0]
    assert d == D_MODEL and seq % CH == 0
    lp = CH + seq
    tm = _row_tile(lp)

    cos2, sin2 = _rope_tables(lp)
    consts = _mixer_constants(tm)

    vec = lambda v: v.reshape(depth, 1, -1).astype(F32)
    w_b = w_in.astype(BF16)
    w_ga = w_b[:, :, MAIN_W:]
    w_ga3 = jnp.concatenate(
        [w_ga, w_ga, w_ga, jnp.zeros((depth, d, LANES - 3 * GLA_GATE_RANK), BF16)], axis=2)
    w2 = gla_gate_w2.astype(F32)
    w2_hi = w2.astype(BF16)
    w2_lo = (w2 - w2_hi.astype(F32)).astype(BF16)
    w2cat = jnp.concatenate(
        [w2_hi, w2_hi, w2_lo, jnp.zeros((depth, LANES - 3 * GLA_GATE_RANK, GLA_QK), BF16)], axis=1)
    pre_mix, gate_b, ret_nw, gla_nw = vec(pre_mix_norm), vec(gla_gate_b), vec(ret_norm_w), vec(gla_norm_w)
    ffn_w = (w_out.astype(F32), ffn_up.astype(F32), ffn_down.astype(F32))
    post_mix, pre_ffn, post_ffn = vec(post_mix_norm), vec(pre_ffn_norm), vec(post_ffn_norm)
    conv_w, conv_b = ffn_conv_w.astype(F32), vec(ffn_conv_b)
    h = x.astype(F32)
    for l in range(depth):
        outs = _proj(h, meta_tokens.astype(F32) if l == 0 else None, pre_mix, w_b, w_ga3, w2cat, gate_b,
                     cos2, sin2, consts["zeta_t"], ffn_w, l, tm)
        if l == 0:
            h = outs[-1]
        wo_b, up_b, dn_b = outs[5:8]
        merged = _mixer(*outs[:5], ret_nw, gla_nw, consts, l)
        h = _ffn(merged, h, wo_b, post_mix, pre_ffn, up_b, conv_w, conv_b, dn_b, post_ffn, l, tm,
                 last=(l == depth - 1))
    return h
```

```python
import functools

import numpy as np
import jax
import jax.numpy as jnp
from jax import lax
from jax.experimental import pallas as pl
from jax.experimental.pallas import tpu as pltpu

D_MODEL = 1024
N_META = 16
RET_HEADS = 4
RET_DK = 128
RET_DV = 128
GLA_HEADS = 4
GLA_DK = 64
GLA_DV = 128
GLA_GATE_RANK = 16
GLA_TAU = 16.0
D_FF = 2816
CONV_W = 3
ROPE_BASE = 10000.0
EPS = 1e-6

RET_QK = RET_HEADS * RET_DK
RET_V = RET_HEADS * RET_DV
GLA_QK = GLA_HEADS * GLA_DK
GLA_V = GLA_HEADS * GLA_DV
D_MIX = RET_V + GLA_V
IN_WIDTH = 2 * RET_QK + 2 * RET_V + 2 * GLA_QK + 2 * GLA_V + GLA_GATE_RANK

LANES = 128
SUBLANES = 8
CH = 128
SUB_ROWS = CH // 2
PAD = CH - N_META
C_RQ, C_RK, C_RV, C_RG = 0, RET_QK, 2 * RET_QK, 2 * RET_QK + RET_V
C_GQ = C_RG + RET_V
C_GK, C_GV = C_GQ + GLA_QK, C_GQ + 2 * GLA_QK
C_GR = C_GV + GLA_V
MAIN_W = C_GR + GLA_V
MIXER_CHUNKS_PER_STEP = 5
FF_TILE = 256
FFN_PARTS = 2
VMEM_LIMIT = 56 * 1024 * 1024
LOG2E = float(np.log2(np.e))

BF16 = jnp.bfloat16
F32 = jnp.float32

_NT = (((1,), (1,)), ((), ()))
_TN = (((0,), (0,)), ((), ()))


def _rms(x):
    return x * lax.rsqrt(jnp.mean(x * x, axis=-1, keepdims=True) + EPS)


def _silu(g):
    return g * (1.0 / (1.0 + jnp.exp(-g)))


def _const_spec(shape):
    nd = len(shape)
    return pl.BlockSpec(shape, lambda *_: (0,) * nd, pipeline_mode=pl.Buffered(1))


def _layer_spec(arr, l):
    nd = arr.ndim - 1
    return pl.BlockSpec((None,) + arr.shape[1:], lambda *_: (l,) + (0,) * nd, pipeline_mode=pl.Buffered(1))


def _proj_tile(tm, h_ref, nw_ref, w_ref, wga_ref, w2_ref, gb_ref, cos_ref, sin_ref, zeta_ref,
               wo_f32_ref, up_f32_ref, dn_f32_ref,
               ret_ref, gate_ref, gqk_ref, gv_ref, la_ref, wo_b16_ref, up_b16_ref, dn_b16_ref):
    j = pl.program_id(1)
    hm = tm // 2
    parts = (slice(0, hm), slice(hm, tm))
    hn = {}

    def norm(rs):
        hn[rs.start] = (_rms(h_ref[0, rs]) * nw_ref[...]).astype(BF16)

    def cols(rs, c0, width):
        return jnp.dot(hn[rs.start], w_ref[:, c0:c0 + width], preferred_element_type=F32)

    def rope(rs, t, h):
        x = t[:, h * RET_DK:(h + 1) * RET_DK]
        return x * cos_ref[rs, :] + pltpu.roll(x, RET_DK // 2, 1) * sin_ref[rs, :]

    def put_q(rs, q):
        for h in range(RET_HEADS):
            ret_ref[0, rs, h * RET_DK:(h + 1) * RET_DK] = rope(rs, q, h).astype(BF16)

    def put_k(rs, k):
        for h in range(RET_HEADS):
            sl = slice(h * RET_DK, (h + 1) * RET_DK)
            kr = rope(rs, k, h)
            ret_ref[0, rs, RET_QK + h * RET_DK:RET_QK + (h + 1) * RET_DK] = kr.astype(BF16)
            ret_ref[0, rs, 2 * RET_QK + h * RET_DK:2 * RET_QK + (h + 1) * RET_DK] = (
                kr * zeta_ref[rs, sl]).astype(BF16)

    def put_v(rs, v):
        ret_ref[0, rs, 3 * RET_QK:] = v.astype(BF16)

    def put_rg(rs, g):
        gate_ref[0, rs, :RET_V] = _silu(g)

    def put_gr(rs, g):
        gate_ref[0, rs, RET_V:] = _silu(g)

    def put_gqk(rs, t):
        gqk_ref[0, rs, :GLA_QK] = t[:, :GLA_QK] * (GLA_DK ** -0.5)
        gqk_ref[0, rs, GLA_QK:] = t[:, GLA_QK:]

    def put_gv(rs, v):
        gv_ref[0, rs] = v.astype(BF16)

    def put_la(rs, g):
        g_hi = g.astype(BF16)
        g_lo = (g - g_hi.astype(F32)).astype(BF16)
        lane = lax.broadcasted_iota(jnp.int32, (1, LANES), 1)
        lo_part = (lane >= GLA_GATE_RANK) & (lane < 2 * GLA_GATE_RANK)
        z = jnp.dot(jnp.where(lo_part, g_lo, g_hi), w2_ref[...], preferred_element_type=F32) + gb_ref[...]
        la = (jnp.minimum(z, 0.0) - jnp.log(1.0 + jnp.exp(-jnp.abs(z)))) * (LOG2E / GLA_TAU)
        row = j * tm + rs.start + lax.broadcasted_iota(jnp.int32, (hm, 1), 0)
        la_ref[0, rs] = jnp.where(row >= PAD, la, 0.0)

    def cast_weights():
        for src, dst in ((wo_f32_ref, wo_b16_ref), (up_f32_ref, up_b16_ref), (dn_f32_ref, dn_b16_ref)):
            dst[...] = src[...].astype(BF16)

    def stages(rs):
        return [
            (lambda: jnp.dot(hn[rs.start], wga_ref[...], preferred_element_type=F32), put_la),
            (lambda: cols(rs, C_RQ, RET_QK), put_q),
            (lambda: cols(rs, C_RK, RET_QK), put_k),
            (lambda: cols(rs, C_RG, RET_V), put_rg),
            (lambda: cols(rs, C_GR, GLA_V), put_gr),
            (lambda: cols(rs, C_GQ, 2 * GLA_QK), put_gqk),
            (lambda: cols(rs, C_RV, RET_V), put_v),
            (lambda: cols(rs, C_GV, GLA_V), put_gv),
        ]

    per_part = [stages(rs) for rs in parts]
    order = [(rs, st) for group in zip(*per_part) for rs, st in zip(parts, group)]
    pending = None
    for rs, (matmul, epilogue) in order:
        if rs.start not in hn:
            norm(rs)
        res = matmul()
        if pending is not None:
            pending()
        pending = functools.partial(epilogue, rs, res)
    cast_weights()
    pending()


def _proj_kernel(tm, *refs):
    _proj_tile(tm, *refs)


def _proj_first_kernel(tm, x_ref, meta_ref, *refs):
    h_ref = refs[-1]
    j = pl.program_id(1)

    @pl.when(j == 0)
    def _():
        h_ref[0, :PAD] = jnp.zeros((PAD, D_MODEL), F32)
        h_ref[0, PAD:CH] = meta_ref[...]
        h_ref[0, CH:] = x_ref[:tm - CH]

    @pl.when(j > 0)
    def _():
        h_ref[0] = x_ref[...]

    _proj_tile(tm, h_ref, *refs[:-1])


def _x_window(tm, d):
    return pl.BlockSpec((None, pl.Element(tm), pl.Element(d)),
                        lambda i, j: (i, pl.multiple_of(jnp.maximum(j * tm - CH, 0), CH), 0))


def _slab_specs(arr, l, axis, unit, steps, nt):
    n_units = arr.shape[axis] // unit
    n_slabs = max(s for s in range(1, n_units + 1) if n_units % s == 0 and s <= steps)
    size = arr.shape[axis] // n_slabs
    shape = tuple(size if a == axis else arr.shape[a] for a in (1, 2))

    def pos(i, j):
        s = jnp.minimum(i * nt + j, n_slabs - 1)
        return (s, 0) if axis == 1 else (0, s)

    return (pl.BlockSpec((None,) + shape, lambda i, j: (l,) + pos(i, j)),
            pl.BlockSpec(shape, pos),
            jax.ShapeDtypeStruct(arr.shape[1:], BF16))


def _proj(h_or_x, meta, nw, w, wga, w2cat, gb, cos2, sin2, zeta_t, ffn_w, l, tm):
    first = meta is not None
    b, d = h_or_x.shape[0], h_or_x.shape[2]
    lp = h_or_x.shape[1] + (CH if first else 0)
    nt = lp // tm
    row = lambda width: pl.BlockSpec((1, tm, width), lambda i, j: (i, j, 0))
    tab = pl.BlockSpec((tm, RET_DK), lambda i, j: (j, 0))
    w_main = pl.BlockSpec((None, d, MAIN_W), lambda i, j: (l, 0, 0), pipeline_mode=pl.Buffered(1))
    slabs = [_slab_specs(ffn_w[0], l, 1, 4 * SUBLANES, b * nt, nt),
             _slab_specs(ffn_w[1], l, 2, 2 * LANES, b * nt, nt),
             _slab_specs(ffn_w[2], l, 1, 4 * SUBLANES, b * nt, nt)]
    lead = [_x_window(tm, d), _const_spec(meta.shape)] if first else [row(d)]
    in_specs = lead + [_layer_spec(nw, l), w_main, _layer_spec(wga, l), _layer_spec(w2cat, l), _layer_spec(gb, l),
                       tab, tab, _const_spec(zeta_t.shape)] + [s[0] for s in slabs]
    out_specs = ([row(4 * RET_QK), row(RET_V + GLA_V), row(2 * GLA_QK), row(GLA_V), row(GLA_QK)]
                 + [s[1] for s in slabs])
    out_shape = [jax.ShapeDtypeStruct((b, lp, 4 * RET_QK), BF16),
                 jax.ShapeDtypeStruct((b, lp, RET_V + GLA_V), F32),
                 jax.ShapeDtypeStruct((b, lp, 2 * GLA_QK), F32),
                 jax.ShapeDtypeStruct((b, lp, GLA_V), BF16),
                 jax.ShapeDtypeStruct((b, lp, GLA_QK), F32)] + [s[2] for s in slabs]
    if first:
        out_specs.append(row(d))
        out_shape.append(jax.ShapeDtypeStruct((b, lp, d), F32))
    args = ([h_or_x, meta] if first else [h_or_x]) + [nw, w, wga, w2cat, gb, cos2, sin2, zeta_t, *ffn_w]
    return pl.pallas_call(
        functools.partial(_proj_first_kernel if first else _proj_kernel, tm),
        grid=(b, nt),
        in_specs=in_specs,
        out_specs=out_specs,
        out_shape=out_shape,
        compiler_params=pltpu.CompilerParams(dimension_semantics=("arbitrary", "arbitrary"),
                                             vmem_limit_bytes=VMEM_LIMIT),
        name="proj_first" if first else "proj",
    )(*args)


def _level_halves():
    out, m = [], SUB_ROWS // 2
    while m >= 1:
        out.append(m)
        m //= 2
    return tuple(out)


def _mixer_constants(tm):
    c = CH
    idx = np.arange(c, dtype=np.float64)
    gam = 1.0 - 2.0 ** (-5.0 - np.arange(RET_HEADS, dtype=np.float64))
    kscale = RET_DK ** -0.5
    diff = idx[:, None] - idx[None, :]
    dmat = np.where(diff >= 0, gam[:, None, None] ** np.maximum(diff, 0.0), 0.0) * kscale
    zeta = np.repeat((gam[:, None] ** (c - 1.0 - idx)[None, :]).T, RET_DK, axis=1) * kscale
    xi = np.repeat((gam[:, None] ** (idx + 1.0)[None, :]).T, RET_DK, axis=1)
    gch = tuple(float(g ** c) for g in gam)
    s = np.arange(SUB_ROWS)
    t = np.arange(2 * SUB_ROWS) % SUB_ROWS
    ml = [(s[:, None] == t[None, :]).astype(np.float64)]
    for m in _level_halves():
        blk_s, blk_t = s // (2 * m), t // (2 * m)
        up_s, up_t = (s % (2 * m)) >= m, (t % (2 * m)) >= m
        ml.append(((blk_s[:, None] == blk_t[None, :]) & up_s[:, None] & (~up_t)[None, :]).astype(np.float64))
    mlev = np.stack(ml, axis=0)
    tril = np.tril(np.ones((c, c)))
    return dict(dmat=jnp.asarray(dmat, F32), zeta_t=jnp.asarray(np.tile(zeta, (tm // c, 1)), F32),
                xi=jnp.asarray(xi, F32), gch=gch, tril=jnp.asarray(tril, BF16), mlev=jnp.asarray(mlev, F32))


def _block_ref(cum, m):
    parts = []
    for r0 in range(0, cum.shape[0], 2 * m):
        parts.append(jnp.broadcast_to(cum[r0 + m - 1:r0 + m, :], (2 * m, cum.shape[1])))
    return parts[0] if len(parts) == 1 else jnp.concatenate(parts, axis=0)


def _mixer_kernel(gch, nb, n_sub,
                  ret_ref, gate_ref, gqk_ref, gv_ref, la_ref, rnw_ref, gnw_ref,
                  dmat_ref, xi_ref, tril_ref, mlev_ref,
                  out_ref, sret_ref, sgla_ref):
    j = pl.program_id(0)

    @pl.when(j == 0)
    def _():
        sret_ref[...] = jnp.zeros_like(sret_ref)
        sgla_ref[...] = jnp.zeros_like(sgla_ref)

    row = lax.broadcasted_iota(jnp.int32, (CH, 1), 0)
    lane = lax.broadcasted_iota(jnp.int32, (1, GLA_QK), 1)
    even = (lane % LANES) < GLA_DK
    tril = tril_ref[...]

    def retention(rs, b, h):
        sl = slice(h * RET_DK, (h + 1) * RET_DK)
        q = ret_ref[b, rs,sl]
        k = ret_ref[b, rs,RET_QK + h * RET_DK:RET_QK + (h + 1) * RET_DK]
        kz = ret_ref[b, rs,2 * RET_QK + h * RET_DK:2 * RET_QK + (h + 1) * RET_DK]
        vb = ret_ref[b, rs,3 * RET_QK + h * RET_DV:3 * RET_QK + (h + 1) * RET_DV]
        sc = lax.dot_general(q, k, _NT, preferred_element_type=F32) * dmat_ref[h]
        st = sret_ref[b, h]
        o = (jnp.dot(sc.astype(BF16), vb, preferred_element_type=F32)
             + jnp.dot(q, st.astype(BF16), preferred_element_type=F32) * xi_ref[:, sl])
        sret_ref[b, h] = st * gch[h] + lax.dot_general(kz, vb, _TN, preferred_element_type=F32)
        d = o - jnp.mean(o, axis=-1, keepdims=True)
        y = d * lax.rsqrt(jnp.mean(d * d, axis=-1, keepdims=True) + EPS) * rnw_ref[:, sl]
        out_ref[b, rs, sl] = (y * gate_ref[b, rs, sl]).astype(out_ref.dtype)

    def gla_prepare(rs, b):
        la = la_ref[b, rs]
        la_hi = la.astype(BF16)
        r1 = la - la_hi.astype(F32)
        la_mid = r1.astype(BF16)
        la_lo = (r1 - la_mid.astype(F32)).astype(BF16)
        cum = (jnp.dot(tril, la_hi, preferred_element_type=F32)
               + jnp.dot(tril, la_mid, preferred_element_type=F32)
               + jnp.dot(tril, la_lo, preferred_element_type=F32))
        qg = gqk_ref[b, rs,:GLA_QK]
        kg = gqk_ref[b, rs,GLA_QK:]
        return dict(la=la, cum=cum, qg=qg, kg_e=jnp.where(even, kg, 0.0), kg_o=jnp.where(even, 0.0, kg),
                    la_up=pltpu.roll(la, CH - 1, 0), la_dn=pltpu.roll(la, 1, 0),
                    acc={}, top={})

    levels = (0,) + _level_halves()
    subs = (slice(0, SUB_ROWS), slice(SUB_ROWS, CH))

    def head_keys(p, rows, pls, e):
        ke, ko = p["kg_e"][rows, pls], p["kg_o"][rows, pls]
        if e is not None:
            ke, ko = ke * e, ko * e
        return jnp.concatenate([ke.astype(BF16), ko.astype(BF16)], axis=0)

    def gla_level(p, pr, lv):
        pls = slice(pr * LANES, (pr + 1) * LANES)
        la, cum, qg = (p[n][:, pls] for n in ("la", "cum", "qg"))
        m = levels[lv]
        if m == 0:
            e = None
            qh = qg.astype(BF16)
        else:
            if m >= SUBLANES // 2:
                ex = -jnp.abs(cum - _block_ref(cum, m))
            elif m == 2:
                pos = row % 4
                ex = jnp.where(pos == 0, p["la_up"][:, pls],
                               jnp.where(pos == 2, la, jnp.where(pos == 3, la + p["la_dn"][:, pls], 0.0)))
            else:
                ex = jnp.where(row % 2 == 1, la, 0.0)
            e = jnp.exp2(ex)
            qh = (qg * e).astype(BF16)
        keys = jnp.concatenate([head_keys(p, rows, pls, None if e is None else e[rows]) for rows in subs], axis=0)
        sc = lax.dot_general(qh, keys, _NT, preferred_element_type=F32)
        for u, rows in enumerate(subs):
            term = sc[rows, u * 2 * SUB_ROWS:(u + 1) * 2 * SUB_ROWS] * mlev_ref[lv]
            p["acc"][pr, u] = term if lv == 0 else p["acc"][pr, u] + term

    def gla_cross(p, pr):
        pls = slice(pr * LANES, (pr + 1) * LANES)
        cum = p["cum"][:, pls]
        cref = cum[SUB_ROWS - 1:SUB_ROWS]
        qh = (p["qg"][subs[1], pls] * jnp.exp2(cum[subs[1]] - cref)).astype(BF16)
        p["top"][pr] = lax.dot_general(qh, head_keys(p, subs[0], pls, jnp.exp2(cref - cum[subs[0]])), _NT,
                                       preferred_element_type=F32)

    def gla_decays(p):
        cum = p["cum"]
        last = cum[CH - 1:CH, :]
        qd = p["qg"] * jnp.exp2(cum)
        dk = jnp.exp2(last - cum)
        p.update(qd=(jnp.where(even, qd, 0.0).astype(BF16), jnp.where(even, 0.0, qd).astype(BF16)),
                 kd=((p["kg_e"] * dk).astype(BF16), (p["kg_o"] * dk).astype(BF16)),
                 dl=jnp.exp2(last))

    def gla_finish(rs, b, p, pr):
        pls = slice(pr * LANES, (pr + 1) * LANES)
        st = sgla_ref[b, :, pls]
        stb = st.astype(BF16)
        vbs = [gv_ref[b, rs, (2 * pr + t) * GLA_DV:(2 * pr + t + 1) * GLA_DV] for t in range(2)]
        vst = [jnp.concatenate([vbs[0][rows], vbs[1][rows]], axis=0) for rows in subs]
        vst_cross = jnp.concatenate([vst[1], vst[0]], axis=0)
        for t in range(2):
            h = 2 * pr + t
            sl = slice(h * GLA_DV, (h + 1) * GLA_DV)
            osl = slice(RET_V + h * GLA_DV, RET_V + (h + 1) * GLA_DV)
            mine = even[:, :LANES] if t == 0 else ~even[:, :LANES]
            a0, a1, ax = (jnp.where(mine, a, 0.0).astype(BF16)
                          for a in (p["acc"][pr, 0], p["acc"][pr, 1], p["top"][pr]))
            o = (jnp.concatenate([jnp.dot(a0, vst[0], preferred_element_type=F32),
                                  jnp.dot(jnp.concatenate([a1, ax], axis=1), vst_cross,
                                          preferred_element_type=F32)], axis=0)
                 + lax.dot_general(p["qd"][t][:, pls], stb, _NT, preferred_element_type=F32))
            y = o * lax.rsqrt(jnp.mean(o * o, axis=-1, keepdims=True) + EPS) * gnw_ref[:, sl]
            out_ref[b, rs, osl] = (y * gate_ref[b, rs, osl]).astype(out_ref.dtype)
        kvt = lax.dot_general(jnp.concatenate(vbs, axis=0),
                              jnp.concatenate([p["kd"][0][:, pls], p["kd"][1][:, pls]], axis=0),
                              _TN, preferred_element_type=F32)
        sgla_ref[b, :, pls] = st * p["dl"][:, pls] + kvt

    fillers = []
    for c in range(n_sub):
        rs = pl.ds(c * CH, CH)
        preps = [gla_prepare(rs, b) for b in range(nb)]
        for b in range(nb):
            fillers += [functools.partial(retention, rs, b, h) for h in range(RET_HEADS)]
        for b in range(nb):
            for pr in range(GLA_HEADS // 2):
                for lv in range(len(levels)):
                    gla_level(preps[b], pr, lv)
                    if fillers:
                        fillers.pop(0)()
                gla_cross(preps[b], pr)
            gla_decays(preps[b])
            fillers += [functools.partial(gla_finish, rs, b, preps[b], pr) for pr in range(GLA_HEADS // 2)]
    for f in fillers:
        f()


def _mixer(ret, gate, gqk, gv, la, rnw, gnw, consts, l):
    b, lp, _ = gate.shape
    n_chunks = lp // CH
    n_sub = max(g for g in range(1, MIXER_CHUNKS_PER_STEP + 1) if n_chunks % g == 0)
    rows = n_sub * CH

    def blk(arr):
        return pl.BlockSpec((b, rows, arr.shape[2]), lambda j: (0, j, 0))

    tabs = [consts[k] for k in ("dmat", "xi", "tril", "mlev")]
    return pl.pallas_call(
        functools.partial(_mixer_kernel, consts["gch"], b, n_sub),
        grid=(n_chunks // n_sub,),
        in_specs=[blk(ret), blk(gate), blk(gqk), blk(gv), blk(la), _layer_spec(rnw, l), _layer_spec(gnw, l)]
        + [_const_spec(t.shape) for t in tabs],
        out_specs=pl.BlockSpec((b, rows, D_MIX), lambda j: (0, j, 0)),
        out_shape=jax.ShapeDtypeStruct((b, lp, D_MIX), BF16),
        scratch_shapes=[pltpu.VMEM((b, RET_HEADS, RET_DK, RET_DV), F32),
                        pltpu.VMEM((b, GLA_DV, GLA_QK), F32)],
        compiler_params=pltpu.CompilerParams(dimension_semantics=("arbitrary",),
                                             vmem_limit_bytes=VMEM_LIMIT),
        name="mixer",
    )(ret, gate, gqk, gv, la, rnw, gnw, *tabs)


def _gelu_tanh(x):
    return 0.5 * x * (1.0 + jnp.tanh(np.sqrt(2.0 / np.pi) * (x + 0.044715 * (x * x * x))))


def _ffn_part_bounds(tm):
    unit = 2 * SUBLANES
    n_units = tm // unit

    def split(n, k):
        if k == 1:
            return [n]
        for delta in range(n):
            for first in (n // k - delta, n // k + delta):
                if 0 < first <= n - (k - 1) and first % 4:
                    rest = split(n - first, k - 1)
                    if all(r % 4 for r in rest):
                        return [first] + rest
        return [n // k + (1 if t < n % k else 0) for t in range(k)]

    sizes = split(n_units, FFN_PARTS)
    return [sum(sizes[:t]) * unit for t in range(FFN_PARTS + 1)]


def _ffn_tile(tm, m_ref, h_ref, wout_ref, pmn_ref, pfn_ref, up_ref, cw_ref, cb_ref, down_ref, pon_ref,
              halo_ref, act_ref, perm_ref):
    i = pl.program_id(1)

    @pl.when(i == 0)
    def _():
        halo_ref[0:2 * SUBLANES] = jnp.zeros((2 * SUBLANES, 2 * D_FF), F32)

    bounds = _ffn_part_bounds(tm)
    parts = [slice(bounds[t], bounds[t + 1]) for t in range(FFN_PARTS)]
    strand = [(p.stop - p.start) // SUBLANES for p in parts]
    n_slabs = D_MODEL // LANES
    sub = lax.broadcasted_iota(jnp.int32, (SUBLANES, 1), 0)

    def to_strands(t, x):
        for k in range(n_slabs):
            perm_ref[k, parts[t], :] = x[:, k * LANES:(k + 1) * LANES]
        return jnp.concatenate(
            [jnp.concatenate([perm_ref[k, pl.ds(parts[t].start + v, SUBLANES, stride=strand[t]), :]
                              for k in range(n_slabs)], axis=1) for v in range(strand[t])], axis=0)

    def from_strands(t, val, p0):
        for kk in range(val.shape[1] // LANES):
            for v in range(strand[t]):
                perm_ref[p0 // LANES + kk, pl.ds(parts[t].start + v, SUBLANES, stride=strand[t]), :] = (
                    val[v * SUBLANES:(v + 1) * SUBLANES, kk * LANES:(kk + 1) * LANES])

    m = [jnp.dot(m_ref[0, rows], wout_ref[...], preferred_element_type=F32) for rows in parts]
    h1, hn = [], []
    for t in range(FFN_PARTS):
        x = h_ref[0, parts[t]] + _rms(m[t]) * pmn_ref[...]
        h1.append(x)
        hn.append(to_strands(t, _rms(x) * pfn_ref[...]).astype(BF16))

    def conv(t, c0):
        cs = slice(c0, c0 + FF_TILE)
        u = jnp.dot(hn[t], up_ref[:, cs], preferred_element_type=F32)
        rows = u.shape[0]
        prev2 = halo_ref[2 * SUBLANES * t:2 * SUBLANES * t + SUBLANES, cs]
        prev1 = halo_ref[2 * SUBLANES * t + SUBLANES:2 * SUBLANES * (t + 1), cs]
        nxt = 2 * SUBLANES * ((t + 1) % FFN_PARTS)
        halo_ref[nxt:nxt + 2 * SUBLANES, cs] = u[rows - 2 * SUBLANES:]
        top1 = jnp.where(sub < 1, pltpu.roll(prev1, 1, 0), pltpu.roll(u[rows - SUBLANES:], 1, 0))
        top2 = jnp.where(sub < 1, pltpu.roll(prev2, 1, 0),
                         pltpu.roll(u[rows - 2 * SUBLANES:rows - SUBLANES], 1, 0))
        s1 = jnp.concatenate([top1, u[:rows - SUBLANES]], axis=0)
        s2 = jnp.concatenate([top2, top1, u[:rows - 2 * SUBLANES]], axis=0)
        return (cb_ref[:, cs] + u * cw_ref[2:3, cs] + s1 * cw_ref[1:2, cs] + s2 * cw_ref[0:1, cs])

    def act_chunk(t, c0):
        a = conv(t, c0)
        g = conv(t, D_FF + c0)
        act_ref[parts[t], c0:c0 + FF_TILE] = (_gelu_tanh(a) * g).astype(BF16)

    def down_piece(t, p0):
        from_strands(t, jnp.dot(act_ref[parts[t], :], down_ref[:, p0:p0 + 2 * LANES],
                                preferred_element_type=F32), p0)

    chunks = list(range(0, D_FF, FF_TILE))
    for t in range(FFN_PARTS):
        pieces = list(range(0, D_MODEL, 2 * LANES)) if t > 0 else []
        for n, c0 in enumerate(chunks):
            act_chunk(t, c0)
            if pieces and n % 2 == 1:
                down_piece(t - 1, pieces.pop(0))
        for p0 in pieces:
            down_piece(t - 1, p0)
    for p0 in range(0, D_MODEL, 2 * LANES):
        down_piece(FFN_PARTS - 1, p0)
    h2 = []
    for t in range(FFN_PARTS):
        f = jnp.concatenate([perm_ref[k, parts[t], :] for k in range(n_slabs)], axis=1)
        h2.append(h1[t] + _rms(f) * pon_ref[...])
    h2 = jnp.concatenate(h2, axis=0)
    row = i * tm + lax.broadcasted_iota(jnp.int32, (tm, 1), 0)
    return jnp.where(row >= PAD, h2, 0.0)


def _ffn_kernel(tm, *refs):
    out_ref, halo_ref, act_ref, f_ref = refs[-4:]
    out_ref[0] = _ffn_tile(tm, *refs[:-4], halo_ref, act_ref, f_ref)


def _ffn_last_kernel(tm, *refs):
    out_ref, halo_ref, act_ref, f_ref, obuf_ref, osem = refs[-6:]
    b, i = pl.program_id(0), pl.program_id(1)
    nt = pl.num_programs(1)
    step = b * nt + i
    slot = step % 2

    def first_copy(s, bb):
        return pltpu.make_async_copy(obuf_ref.at[s, pl.ds(CH, tm - CH)],
                                     out_ref.at[bb, pl.ds(0, tm - CH)], osem.at[s])

    def tile_copy(s, bb, ii):
        return pltpu.make_async_copy(obuf_ref.at[s], out_ref.at[bb, pl.ds(ii * tm - CH, tm)], osem.at[s])

    def wait_step(s, bb, ii):
        @pl.when(ii == 0)
        def _():
            first_copy(s, bb).wait()

        @pl.when(ii > 0)
        def _():
            tile_copy(s, bb, ii).wait()

    res = _ffn_tile(tm, *refs[:-6], halo_ref, act_ref, f_ref)

    @pl.when(step >= 2)
    def _():
        prev = step - 2
        wait_step(slot, prev // nt, prev % nt)

    obuf_ref[slot] = res

    @pl.when(i == 0)
    def _():
        first_copy(slot, b).start()

    @pl.when(i > 0)
    def _():
        tile_copy(slot, b, i).start()

    @pl.when(step == pl.num_programs(0) * nt - 1)
    def _():
        wait_step(1 - slot, (step - 1) // nt, (step - 1) % nt)
        wait_step(slot, b, i)


def _ffn(merged, h, wout, pmn, pfn, up, cw, cb, down, pon, l, tm, last):
    b, lp, d = h.shape
    assert lp // tm >= 2
    row = lambda width: pl.BlockSpec((1, tm, width), lambda i, j: (i, j, 0))
    assert tm % (2 * SUBLANES) == 0 and tm // (2 * SUBLANES) >= FFN_PARTS
    scratch = [pltpu.VMEM((FFN_PARTS * 2 * SUBLANES, 2 * D_FF), F32), pltpu.VMEM((tm, D_FF), BF16),
               pltpu.VMEM((d // LANES, tm, LANES), F32)]
    if last:
        scratch += [pltpu.VMEM((2, tm, d), F32), pltpu.SemaphoreType.DMA((2,))]
    return pl.pallas_call(
        functools.partial(_ffn_last_kernel if last else _ffn_kernel, tm),
        grid=(b, lp // tm),
        in_specs=[row(D_MIX), row(d)] + [_const_spec(a.shape) if a.ndim == 2 else _layer_spec(a, l)
                                         for a in (wout, pmn, pfn, up, cw, cb, down, pon)],
        out_specs=pl.BlockSpec(memory_space=pl.ANY) if last else row(d),
        out_shape=jax.ShapeDtypeStruct((b, lp - CH, d) if last else (b, lp, d), F32),
        scratch_shapes=scratch,
        compiler_params=pltpu.CompilerParams(
            dimension_semantics=("arbitrary", "arbitrary") if last else ("parallel", "arbitrary"),
            vmem_limit_bytes=VMEM_LIMIT),
        name="ffn_last" if last else "ffn",
    )(merged, h, wout, pmn, pfn, up, cw, cb, down, pon)


def _rope_tables(lp):
    half = RET_DK // 2
    inv = ROPE_BASE ** (-np.arange(half, dtype=np.float64) / half)
    ang = (np.arange(lp, dtype=np.float64) - PAD)[:, None] * inv[None, :]
    c, s = np.cos(ang), np.sin(ang)
    return (jnp.asarray(np.concatenate([c, c], axis=-1), F32),
            jnp.asarray(np.concatenate([-s, s], axis=-1), F32))


def _row_tile(lp):
    for tm in (640, 512, 384, 256, 128):
        if lp % tm == 0:
            return tm
    raise ValueError(f"padded length {lp} is not a multiple of {CH}")


def kernel(x, meta_tokens, pre_mix_norm, w_in, gla_gate_w2, gla_gate_b, ret_norm_w, gla_norm_w, w_out,
           post_mix_norm, pre_ffn_norm, ffn_up, ffn_conv_w, ffn_conv_b, ffn_down, post_ffn_norm):
    bsz, seq, d = x.shape
    depth = w_in.shape[0]
    assert d == D_MODEL and seq % CH == 0
    lp = CH + seq
    tm = _row_tile(lp)

    cos2, sin2 = _rope_tables(lp)
    consts = _mixer_constants(tm)

    vec = lambda v: v.reshape(depth, 1, -1).astype(F32)
    w_b = w_in.astype(BF16)
    w_ga = w_b[:, :, MAIN_W:]
    w_ga3 = jnp.concatenate(
        [w_ga, w_ga, w_ga, jnp.zeros((depth, d, LANES - 3 * GLA_GATE_RANK), BF16)], axis=2)
    w2 = gla_gate_w2.astype(F32)
    w2_hi = w2.astype(BF16)
    w2_lo = (w2 - w2_hi.astype(F32)).astype(BF16)
    w2cat = jnp.concatenate(
        [w2_hi, w2_hi, w2_lo, jnp.zeros((depth, LANES - 3 * GLA_GATE_RANK, GLA_QK), BF16)], axis=1)
    pre_mix, gate_b, ret_nw, gla_nw = vec(pre_mix_norm), vec(gla_gate_b), vec(ret_norm_w), vec(gla_norm_w)
    ffn_w = (w_out.astype(F32), ffn_up.astype(F32), ffn_down.astype(F32))
    post_mix, pre_ffn, post_ffn = vec(post_mix_norm), vec(pre_ffn_norm), vec(post_ffn_norm)
    conv_w, conv_b = ffn_conv_w.astype(F32), vec(ffn_conv_b)
    h = x.astype(F32)
    for l in range(depth):
        outs = _proj(h, meta_tokens.astype(F32) if l == 0 else None, pre_mix, w_b, w_ga3, w2cat, gate_b,
                     cos2, sin2, consts["zeta_t"], ffn_w, l, tm)
        if l == 0:
            h = outs[-1]
        wo_b, up_b, dn_b = outs[5:8]
        merged = _mixer(*outs[:5], ret_nw, gla_nw, consts, l)
        h = _ffn(merged, h, wo_b, post_mix, pre_ffn, up_b, conv_w, conv_b, dn_b, post_ffn, l, tm,
                 last=(l == depth - 1))
    return h
```

```python
import functools

import numpy as np
import jax
import jax.numpy as jnp
from jax import lax
from jax.experimental import pallas as pl
from jax.experimental.pallas import tpu as pltpu

D_MODEL = 1024
N_META = 16
RET_HEADS = 4
RET_DK = 128
RET_DV = 128
GLA_HEADS = 4
GLA_DK = 64
GLA_DV = 128
GLA_GATE_RANK = 16
GLA_TAU = 16.0
D_FF = 2816
CONV_W = 3
ROPE_BASE = 10000.0
EPS = 1e-6

RET_QK = RET_HEADS * RET_DK
RET_V = RET_HEADS * RET_DV
GLA_QK = GLA_HEADS * GLA_DK
GLA_V = GLA_HEADS * GLA_DV
D_MIX = RET_V + GLA_V
IN_WIDTH = 2 * RET_QK + 2 * RET_V + 2 * GLA_QK + 2 * GLA_V + GLA_GATE_RANK

LANES = 128
SUBLANES = 8
CH = 128
SUB_ROWS = CH // 2
PAD = CH - N_META
C_RQ, C_RK, C_RV, C_RG = 0, RET_QK, 2 * RET_QK, 2 * RET_QK + RET_V
C_GQ = C_RG + RET_V
C_GK, C_GV = C_GQ + GLA_QK, C_GQ + 2 * GLA_QK
C_GR = C_GV + GLA_V
MAIN_W = C_GR + GLA_V
MIXER_CHUNKS_PER_STEP = 5
FF_TILE = 256
FFN_PARTS = 2
VMEM_LIMIT = 56 * 1024 * 1024
LOG2E = float(np.log2(np.e))

BF16 = jnp.bfloat16
F32 = jnp.float32

_NT = (((1,), (1,)), ((), ()))
_TN = (((0,), (0,)), ((), ()))


def _rms(x):
    return x * lax.rsqrt(jnp.mean(x * x, axis=-1, keepdims=True) + EPS)


def _silu(g):
    return g * (1.0 / (1.0 + jnp.exp(-g)))


def _const_spec(shape):
    nd = len(shape)
    return pl.BlockSpec(shape, lambda *_: (0,) * nd, pipeline_mode=pl.Buffered(1))


def _layer_spec(arr, l):
    nd = arr.ndim - 1
    return pl.BlockSpec((None,) + arr.shape[1:], lambda *_: (l,) + (0,) * nd, pipeline_mode=pl.Buffered(1))


def _proj_tile(tm, h_ref, nw_ref, w_ref, wga_ref, w2_ref, gb_ref, cos_ref, sin_ref, zeta_ref,
               wo_f32_ref, up_f32_ref, dn_f32_ref,
               ret_ref, gate_ref, gqk_ref, gv_ref, la_ref, wo_b16_ref, up_b16_ref, dn_b16_ref):
    j = pl.program_id(1)
    hm = tm // 2
    parts = (slice(0, hm), slice(hm, tm))
    hn = {}

    def norm(rs):
        hn[rs.start] = (_rms(h_ref[0, rs]) * nw_ref[...]).astype(BF16)

    def cols(rs, c0, width):
        return jnp.dot(hn[rs.start], w_ref[:, c0:c0 + width], preferred_element_type=F32)

    def rope(rs, t, h):
        x = t[:, h * RET_DK:(h + 1) * RET_DK]
        return x * cos_ref[rs, :] + pltpu.roll(x, RET_DK // 2, 1) * sin_ref[rs, :]

    def put_q(rs, q):
        for h in range(RET_HEADS):
            ret_ref[0, rs, h * RET_DK:(h + 1) * RET_DK] = rope(rs, q, h).astype(BF16)

    def put_k(rs, k):
        for h in range(RET_HEADS):
            sl = slice(h * RET_DK, (h + 1) * RET_DK)
            kr = rope(rs, k, h)
            ret_ref[0, rs, RET_QK + h * RET_DK:RET_QK + (h + 1) * RET_DK] = kr.astype(BF16)
            ret_ref[0, rs, 2 * RET_QK + h * RET_DK:2 * RET_QK + (h + 1) * RET_DK] = (
                kr * zeta_ref[rs, sl]).astype(BF16)

    def put_v(rs, v):
        ret_ref[0, rs, 3 * RET_QK:] = v.astype(BF16)

    def put_rg(rs, g):
        gate_ref[0, rs, :RET_V] = _silu(g)

    def put_gr(rs, g):
        gate_ref[0, rs, RET_V:] = _silu(g)

    def put_gqk(rs, t):
        gqk_ref[0, rs, :GLA_QK] = t[:, :GLA_QK] * (GLA_DK ** -0.5)
        gqk_ref[0, rs, GLA_QK:] = t[:, GLA_QK:]

    def put_gv(rs, v):
        gv_ref[0, rs] = v.astype(BF16)

    def put_la(rs, g):
        g_hi = g.astype(BF16)
        g_lo = (g - g_hi.astype(F32)).astype(BF16)
        lane = lax.broadcasted_iota(jnp.int32, (1, LANES), 1)
        lo_part = (lane >= GLA_GATE_RANK) & (lane < 2 * GLA_GATE_RANK)
        z = jnp.dot(jnp.where(lo_part, g_lo, g_hi), w2_ref[...], preferred_element_type=F32) + gb_ref[...]
        la = (jnp.minimum(z, 0.0) - jnp.log(1.0 + jnp.exp(-jnp.abs(z)))) * (LOG2E / GLA_TAU)
        row = j * tm + rs.start + lax.broadcasted_iota(jnp.int32, (hm, 1), 0)
        la_ref[0, rs] = jnp.where(row >= PAD, la, 0.0)

    def cast_weights():
        for src, dst in ((wo_f32_ref, wo_b16_ref), (up_f32_ref, up_b16_ref), (dn_f32_ref, dn_b16_ref)):
            dst[...] = src[...].astype(BF16)

    def stages(rs):
        return [
            (lambda: jnp.dot(hn[rs.start], wga_ref[...], preferred_element_type=F32), put_la),
            (lambda: cols(rs, C_RQ, RET_QK), put_q),
            (lambda: cols(rs, C_RK, RET_QK), put_k),
            (lambda: cols(rs, C_RG, RET_V), put_rg),
            (lambda: cols(rs, C_GR, GLA_V), put_gr),
            (lambda: cols(rs, C_GQ, 2 * GLA_QK), put_gqk),
            (lambda: cols(rs, C_RV, RET_V), put_v),
            (lambda: cols(rs, C_GV, GLA_V), put_gv),
        ]

    per_part = [stages(rs) for rs in parts]
    order = [(rs, st) for group in zip(*per_part) for rs, st in zip(parts, group)]
    pending = None
    for rs, (matmul, epilogue) in order:
        if rs.start not in hn:
            norm(rs)
        res = matmul()
        if pending is not None:
            pending()
        pending = functools.partial(epilogue, rs, res)
    cast_weights()
    pending()


def _proj_kernel(tm, *refs):
    _proj_tile(tm, *refs)


def _proj_first_kernel(tm, x_ref, meta_ref, *refs):
    h_ref = refs[-1]
    j = pl.program_id(1)

    @pl.when(j == 0)
    def _():
        h_ref[0, :PAD] = jnp.zeros((PAD, D_MODEL), F32)
        h_ref[0, PAD:CH] = meta_ref[...]
        h_ref[0, CH:] = x_ref[:tm - CH]

    @pl.when(j > 0)
    def _():
        h_ref[0] = x_ref[...]

    _proj_tile(tm, h_ref, *refs[:-1])


def _x_window(tm, d):
    return pl.BlockSpec((None, pl.Element(tm), pl.Element(d)),
                        lambda i, j: (i, pl.multiple_of(jnp.maximum(j * tm - CH, 0), CH), 0))


def _slab_specs(arr, l, axis, unit, steps, nt):
    n_units = arr.shape[axis] // unit
    n_slabs = max(s for s in range(1, n_units + 1) if n_units % s == 0 and s <= steps)
    size = arr.shape[axis] // n_slabs
    shape = tuple(size if a == axis else arr.shape[a] for a in (1, 2))

    def pos(i, j):
        s = jnp.minimum(i * nt + j, n_slabs - 1)
        return (s, 0) if axis == 1 else (0, s)

    return (pl.BlockSpec((None,) + shape, lambda i, j: (l,) + pos(i, j)),
            pl.BlockSpec(shape, pos),
            jax.ShapeDtypeStruct(arr.shape[1:], BF16))


def _proj(h_or_x, meta, nw, w, wga, w2cat, gb, cos2, sin2, zeta_t, ffn_w, l, tm):
    first = meta is not None
    b, d = h_or_x.shape[0], h_or_x.shape[2]
    lp = h_or_x.shape[1] + (CH if first else 0)
    nt = lp // tm
    row = lambda width: pl.BlockSpec((1, tm, width), lambda i, j: (i, j, 0))
    tab = pl.BlockSpec((tm, RET_DK), lambda i, j: (j, 0))
    w_main = pl.BlockSpec((None, d, MAIN_W), lambda i, j: (l, 0, 0), pipeline_mode=pl.Buffered(1))
    slabs = [_slab_specs(ffn_w[0], l, 1, 4 * SUBLANES, b * nt, nt),
             _slab_specs(ffn_w[1], l, 2, 2 * LANES, b * nt, nt),
             _slab_specs(ffn_w[2], l, 1, 4 * SUBLANES, b * nt, nt)]
    lead = [_x_window(tm, d), _const_spec(meta.shape)] if first else [row(d)]
    in_specs = lead + [_layer_spec(nw, l), w_main, _layer_spec(wga, l), _layer_spec(w2cat, l), _layer_spec(gb, l),
                       tab, tab, _const_spec(zeta_t.shape)] + [s[0] for s in slabs]
    out_specs = ([row(4 * RET_QK), row(RET_V + GLA_V), row(2 * GLA_QK), row(GLA_V), row(GLA_QK)]
                 + [s[1] for s in slabs])
    out_shape = [jax.ShapeDtypeStruct((b, lp, 4 * RET_QK), BF16),
                 jax.ShapeDtypeStruct((b, lp, RET_V + GLA_V), F32),
                 jax.ShapeDtypeStruct((b, lp, 2 * GLA_QK), F32),
                 jax.ShapeDtypeStruct((b, lp, GLA_V), BF16),
                 jax.ShapeDtypeStruct((b, lp, GLA_QK), F32)] + [s[2] for s in slabs]
    if first:
        out_specs.append(row(d))
        out_shape.append(jax.ShapeDtypeStruct((b, lp, d), F32))
    args = ([h_or_x, meta] if first else [h_or_x]) + [nw, w, wga, w2cat, gb, cos2, sin2, zeta_t, *ffn_w]
    return pl.pallas_call(
        functools.partial(_proj_first_kernel if first else _proj_kernel, tm),
        grid=(b, nt),
        in_specs=in_specs,
        out_specs=out_specs,
        out_shape=out_shape,
        compiler_params=pltpu.CompilerParams(dimension_semantics=("arbitrary", "arbitrary"),
                                             vmem_limit_bytes=VMEM_LIMIT),
        name="proj_first" if first else "proj",
    )(*args)


def _level_halves():
    out, m = [], SUB_ROWS // 2
    while m >= 1:
        out.append(m)
        m //= 2
    return tuple(out)


def _mixer_constants(tm):
    c = CH
    idx = np.arange(c, dtype=np.float64)
    gam = 1.0 - 2.0 ** (-5.0 - np.arange(RET_HEADS, dtype=np.float64))
    kscale = RET_DK ** -0.5
    diff = idx[:, None] - idx[None, :]
    dmat = np.where(diff >= 0, gam[:, None, None] ** np.maximum(diff, 0.0), 0.0) * kscale
    zeta = np.repeat((gam[:, None] ** (c - 1.0 - idx)[None, :]).T, RET_DK, axis=1) * kscale
    xi = np.repeat((gam[:, None] ** (idx + 1.0)[None, :]).T, RET_DK, axis=1)
    gch = tuple(float(g ** c) for g in gam)
    s = np.arange(SUB_ROWS)
    t = np.arange(2 * SUB_ROWS) % SUB_ROWS
    ml = [(s[:, None] == t[None, :]).astype(np.float64)]
    for m in _level_halves():
        blk_s, blk_t = s // (2 * m), t // (2 * m)
        up_s, up_t = (s % (2 * m)) >= m, (t % (2 * m)) >= m
        ml.append(((blk_s[:, None] == blk_t[None, :]) & up_s[:, None] & (~up_t)[None, :]).astype(np.float64))
    mlev = np.stack(ml, axis=0)
    tril = np.tril(np.ones((c, c)))
    return dict(dmat=jnp.asarray(dmat, F32), zeta_t=jnp.asarray(np.tile(zeta, (tm // c, 1)), F32),
                xi=jnp.asarray(xi, F32), gch=gch, tril=jnp.asarray(tril, BF16), mlev=jnp.asarray(mlev, F32))


def _block_ref(cum, m):
    parts = []
    for r0 in range(0, cum.shape[0], 2 * m):
        parts.append(jnp.broadcast_to(cum[r0 + m - 1:r0 + m, :], (2 * m, cum.shape[1])))
    return parts[0] if len(parts) == 1 else jnp.concatenate(parts, axis=0)


def _mixer_kernel(gch, nb, n_sub,
                  ret_ref, gate_ref, gqk_ref, gv_ref, la_ref, rnw_ref, gnw_ref,
                  dmat_ref, xi_ref, tril_ref, mlev_ref,
                  out_ref, sret_ref, sgla_ref):
    j = pl.program_id(0)

    @pl.when(j == 0)
    def _():
        sret_ref[...] = jnp.zeros_like(sret_ref)
        sgla_ref[...] = jnp.zeros_like(sgla_ref)

    row = lax.broadcasted_iota(jnp.int32, (CH, 1), 0)
    lane = lax.broadcasted_iota(jnp.int32, (1, GLA_QK), 1)
    even = (lane % LANES) < GLA_DK
    tril = tril_ref[...]

    def retention(rs, b, h):
        sl = slice(h * RET_DK, (h + 1) * RET_DK)
        q = ret_ref[b, rs,sl]
        k = ret_ref[b, rs,RET_QK + h * RET_DK:RET_QK + (h + 1) * RET_DK]
        kz = ret_ref[b, rs,2 * RET_QK + h * RET_DK:2 * RET_QK + (h + 1) * RET_DK]
        vb = ret_ref[b, rs,3 * RET_QK + h * RET_DV:3 * RET_QK + (h + 1) * RET_DV]
        sc = lax.dot_general(q, k, _NT, preferred_element_type=F32) * dmat_ref[h]
        st = sret_ref[b, h]
        o = (jnp.dot(sc.astype(BF16), vb, preferred_element_type=F32)
             + jnp.dot(q, st.astype(BF16), preferred_element_type=F32) * xi_ref[:, sl])
        sret_ref[b, h] = st * gch[h] + lax.dot_general(kz, vb, _TN, preferred_element_type=F32)
        d = o - jnp.mean(o, axis=-1, keepdims=True)
        y = d * lax.rsqrt(jnp.mean(d * d, axis=-1, keepdims=True) + EPS) * rnw_ref[:, sl]
        out_ref[b, rs, sl] = (y * gate_ref[b, rs, sl]).astype(out_ref.dtype)

    def gla_prepare(rs, b):
        la = la_ref[b, rs]
        la_hi = la.astype(BF16)
        r1 = la - la_hi.astype(F32)
        la_mid = r1.astype(BF16)
        la_lo = (r1 - la_mid.astype(F32)).astype(BF16)
        cum = (jnp.dot(tril, la_hi, preferred_element_type=F32)
               + jnp.dot(tril, la_mid, preferred_element_type=F32)
               + jnp.dot(tril, la_lo, preferred_element_type=F32))
        qg = gqk_ref[b, rs,:GLA_QK]
        kg = gqk_ref[b, rs,GLA_QK:]
        return dict(la=la, cum=cum, qg=qg, kg_e=jnp.where(even, kg, 0.0), kg_o=jnp.where(even, 0.0, kg),
                    la_up=pltpu.roll(la, CH - 1, 0), la_dn=pltpu.roll(la, 1, 0),
                    acc={}, top={})

    levels = (0,) + _level_halves()
    subs = (slice(0, SUB_ROWS), slice(SUB_ROWS, CH))

    def head_keys(p, rows, pls, e):
        ke, ko = p["kg_e"][rows, pls], p["kg_o"][rows, pls]
        if e is not None:
            ke, ko = ke * e, ko * e
        return jnp.concatenate([ke.astype(BF16), ko.astype(BF16)], axis=0)

    def gla_level(p, pr, lv):
        pls = slice(pr * LANES, (pr + 1) * LANES)
        la, cum, qg = (p[n][:, pls] for n in ("la", "cum", "qg"))
        m = levels[lv]
        if m == 0:
            e = None
            qh = qg.astype(BF16)
        else:
            if m >= SUBLANES // 2:
                ex = -jnp.abs(cum - _block_ref(cum, m))
            elif m == 2:
                pos = row % 4
                ex = jnp.where(pos == 0, p["la_up"][:, pls],
                               jnp.where(pos == 2, la, jnp.where(pos == 3, la + p["la_dn"][:, pls], 0.0)))
            else:
                ex = jnp.where(row % 2 == 1, la, 0.0)
            e = jnp.exp2(ex)
            qh = (qg * e).astype(BF16)
        keys = jnp.concatenate([head_keys(p, rows, pls, None if e is None else e[rows]) for rows in subs], axis=0)
        sc = lax.dot_general(qh, keys, _NT, preferred_element_type=F32)
        for u, rows in enumerate(subs):
            term = sc[rows, u * 2 * SUB_ROWS:(u + 1) * 2 * SUB_ROWS] * mlev_ref[lv]
            p["acc"][pr, u] = term if lv == 0 else p["acc"][pr, u] + term

    def gla_cross(p, pr):
        pls = slice(pr * LANES, (pr + 1) * LANES)
        cum = p["cum"][:, pls]
        cref = cum[SUB_ROWS - 1:SUB_ROWS]
        qh = (p["qg"][subs[1], pls] * jnp.exp2(cum[subs[1]] - cref)).astype(BF16)
        p["top"][pr] = lax.dot_general(qh, head_keys(p, subs[0], pls, jnp.exp2(cref - cum[subs[0]])), _NT,
                                       preferred_element_type=F32)

    def gla_decays(p):
        cum = p["cum"]
        last = cum[CH - 1:CH, :]
        qd = p["qg"] * jnp.exp2(cum)
        dk = jnp.exp2(last - cum)
        p.update(qd=(jnp.where(even, qd, 0.0).astype(BF16), jnp.where(even, 0.0, qd).astype(BF16)),
                 kd=((p["kg_e"] * dk).astype(BF16), (p["kg_o"] * dk).astype(BF16)),
                 dl=jnp.exp2(last))

    def gla_finish(rs, b, p, pr):
        pls = slice(pr * LANES, (pr + 1) * LANES)
        st = sgla_ref[b, :, pls]
        stb = st.astype(BF16)
        vbs = [gv_ref[b, rs, (2 * pr + t) * GLA_DV:(2 * pr + t + 1) * GLA_DV] for t in range(2)]
        vst = [jnp.concatenate([vbs[0][rows], vbs[1][rows]], axis=0) for rows in subs]
        vst_cross = jnp.concatenate([vst[1], vst[0]], axis=0)
        for t in range(2):
            h = 2 * pr + t
            sl = slice(h * GLA_DV, (h + 1) * GLA_DV)
            osl = slice(RET_V + h * GLA_DV, RET_V + (h + 1) * GLA_DV)
            mine = even[:, :LANES] if t == 0 else ~even[:, :LANES]
            a0, a1, ax = (jnp.where(mine, a, 0.0).astype(BF16)
                          for a in (p["acc"][pr, 0], p["acc"][pr, 1], p["top"][pr]))
            o = (jnp.concatenate([jnp.dot(a0, vst[0], preferred_element_type=F32),
                                  jnp.dot(jnp.concatenate([a1, ax], axis=1), vst_cross,
                                          preferred_element_type=F32)], axis=0)
                 + lax.dot_general(p["qd"][t][:, pls], stb, _NT, preferred_element_type=F32))
            y = o * lax.rsqrt(jnp.mean(o * o, axis=-1, keepdims=True) + EPS) * gnw_ref[:, sl]
            out_ref[b, rs, osl] = (y * gate_ref[b, rs, osl]).astype(out_ref.dtype)
        kvt = lax.dot_general(jnp.concatenate(vbs, axis=0),
                              jnp.concatenate([p["kd"][0][:, pls], p["kd"][1][:, pls]], axis=0),
                              _TN, preferred_element_type=F32)
        sgla_ref[b, :, pls] = st * p["dl"][:, pls] + kvt

    fillers = []
    for c in range(n_sub):
        rs = pl.ds(c * CH, CH)
        preps = [gla_prepare(rs, b) for b in range(nb)]
        for b in range(nb):
            fillers += [functools.partial(retention, rs, b, h) for h in range(RET_HEADS)]
        for b in range(nb):
            for pr in range(GLA_HEADS // 2):
                for lv in range(len(levels)):
                    gla_level(preps[b], pr, lv)
                    if fillers:
                        fillers.pop(0)()
                gla_cross(preps[b], pr)
            gla_decays(preps[b])
            fillers += [functools.partial(gla_finish, rs, b, preps[b], pr) for pr in range(GLA_HEADS // 2)]
    for f in fillers:
        f()


def _mixer(ret, gate, gqk, gv, la, rnw, gnw, consts, l):
    b, lp, _ = gate.shape
    n_chunks = lp // CH
    n_sub = max(g for g in range(1, MIXER_CHUNKS_PER_STEP + 1) if n_chunks % g == 0)
    rows = n_sub * CH

    def blk(arr):
        return pl.BlockSpec((b, rows, arr.shape[2]), lambda j: (0, j, 0))

    tabs = [consts[k] for k in ("dmat", "xi", "tril", "mlev")]
    return pl.pallas_call(
        functools.partial(_mixer_kernel, consts["gch"], b, n_sub),
        grid=(n_chunks // n_sub,),
        in_specs=[blk(ret), blk(gate), blk(gqk), blk(gv), blk(la), _layer_spec(rnw, l), _layer_spec(gnw, l)]
        + [_const_spec(t.shape) for t in tabs],
        out_specs=pl.BlockSpec((b, rows, D_MIX), lambda j: (0, j, 0)),
        out_shape=jax.ShapeDtypeStruct((b, lp, D_MIX), BF16),
        scratch_shapes=[pltpu.VMEM((b, RET_HEADS, RET_DK, RET_DV), F32),
                        pltpu.VMEM((b, GLA_DV, GLA_QK), F32)],
        compiler_params=pltpu.CompilerParams(dimension_semantics=("arbitrary",),
                                             vmem_limit_bytes=VMEM_LIMIT),
        name="mixer",
    )(ret, gate, gqk, gv, la, rnw, gnw, *tabs)


def _gelu_tanh(x):
    return 0.5 * x * (1.0 + jnp.tanh(np.sqrt(2.0 / np.pi) * (x + 0.044715 * (x * x * x))))


def _ffn_part_bounds(tm):
    unit = 2 * SUBLANES
    n_units = tm // unit

    def split(n, k):
        if k == 1:
            return [n]
        for delta in range(n):
            for first in (n // k - delta, n // k + delta):
                if 0 < first <= n - (k - 1) and first % 4:
                    rest = split(n - first, k - 1)
                    if all(r % 4 for r in rest):
                        return [first] + rest
        return [n // k + (1 if t < n % k else 0) for t in range(k)]

    sizes = split(n_units, FFN_PARTS)
    return [sum(sizes[:t]) * unit for t in range(FFN_PARTS + 1)]


def _ffn_tile(tm, m_ref, h_ref, wout_ref, pmn_ref, pfn_ref, up_ref, cw_ref, cb_ref, down_ref, pon_ref,
              halo_ref, act_ref, perm_ref):
    i = pl.program_id(1)

    @pl.when(i == 0)
    def _():
        halo_ref[0:2 * SUBLANES] = jnp.zeros((2 * SUBLANES, 2 * D_FF), F32)

    bounds = _ffn_part_bounds(tm)
    parts = [slice(bounds[t], bounds[t + 1]) for t in range(FFN_PARTS)]
    strand = [(p.stop - p.start) // SUBLANES for p in parts]
    n_slabs = D_MODEL // LANES
    sub = lax.broadcasted_iota(jnp.int32, (SUBLANES, 1), 0)

    def to_strands(t, x):
        for k in range(n_slabs):
            perm_ref[k, parts[t], :] = x[:, k * LANES:(k + 1) * LANES]
        return jnp.concatenate(
            [jnp.concatenate([perm_ref[k, pl.ds(parts[t].start + v, SUBLANES, stride=strand[t]), :]
                              for k in range(n_slabs)], axis=1) for v in range(strand[t])], axis=0)

    def from_strands(t, val, p0):
        for kk in range(val.shape[1] // LANES):
            for v in range(strand[t]):
                perm_ref[p0 // LANES + kk, pl.ds(parts[t].start + v, SUBLANES, stride=strand[t]), :] = (
                    val[v * SUBLANES:(v + 1) * SUBLANES, kk * LANES:(kk + 1) * LANES])

    m = [jnp.dot(m_ref[0, rows], wout_ref[...], preferred_element_type=F32) for rows in parts]
    h1, hn = [], []
    for t in range(FFN_PARTS):
        x = h_ref[0, parts[t]] + _rms(m[t]) * pmn_ref[...]
        h1.append(x)
        hn.append(to_strands(t, _rms(x) * pfn_ref[...]).astype(BF16))

    def conv(t, c0):
        cs = slice(c0, c0 + FF_TILE)
        u = jnp.dot(hn[t], up_ref[:, cs], preferred_element_type=F32)
        rows = u.shape[0]
        prev2 = halo_ref[2 * SUBLANES * t:2 * SUBLANES * t + SUBLANES, cs]
        prev1 = halo_ref[2 * SUBLANES * t + SUBLANES:2 * SUBLANES * (t + 1), cs]
        nxt = 2 * SUBLANES * ((t + 1) % FFN_PARTS)
        halo_ref[nxt:nxt + 2 * SUBLANES, cs] = u[rows - 2 * SUBLANES:]
        top1 = jnp.where(sub < 1, pltpu.roll(prev1, 1, 0), pltpu.roll(u[rows - SUBLANES:], 1, 0))
        top2 = jnp.where(sub < 1, pltpu.roll(prev2, 1, 0),
                         pltpu.roll(u[rows - 2 * SUBLANES:rows - SUBLANES], 1, 0))
        s1 = jnp.concatenate([top1, u[:rows - SUBLANES]], axis=0)
        s2 = jnp.concatenate([top2, top1, u[:rows - 2 * SUBLANES]], axis=0)
        return (cb_ref[:, cs] + u * cw_ref[2:3, cs] + s1 * cw_ref[1:2, cs] + s2 * cw_ref[0:1, cs])

    def act_chunk(t, c0):
        a = conv(t, c0)
        g = conv(t, D_FF + c0)
        act_ref[parts[t], c0:c0 + FF_TILE] = (_gelu_tanh(a) * g).astype(BF16)

    def down_piece(t, p0):
        from_strands(t, jnp.dot(act_ref[parts[t], :], down_ref[:, p0:p0 + 2 * LANES],
                                preferred_element_type=F32), p0)

    chunks = list(range(0, D_FF, FF_TILE))
    for t in range(FFN_PARTS):
        pieces = list(range(0, D_MODEL, 2 * LANES)) if t > 0 else []
        for n, c0 in enumerate(chunks):
            act_chunk(t, c0)
            if pieces and n % 2 == 1:
                down_piece(t - 1, pieces.pop(0))
        for p0 in pieces:
            down_piece(t - 1, p0)
    for p0 in range(0, D_MODEL, 2 * LANES):
        down_piece(FFN_PARTS - 1, p0)
    h2 = []
    for t in range(FFN_PARTS):
        f = jnp.concatenate([perm_ref[k, parts[t], :] for k in range(n_slabs)], axis=1)
        h2.append(h1[t] + _rms(f) * pon_ref[...])
    h2 = jnp.concatenate(h2, axis=0)
    row = i * tm + lax.broadcasted_iota(jnp.int32, (tm, 1), 0)
    return jnp.where(row >= PAD, h2, 0.0)


def _ffn_kernel(tm, *refs):
    out_ref, halo_ref, act_ref, perm_ref = refs[-4:]
    out_ref[0] = _ffn_tile(tm, *refs[:-4], halo_ref, act_ref, perm_ref)


def _ffn_last_kernel(tm, *refs):
    out_ref, halo_ref, act_ref, perm_ref, obuf_ref, osem = refs[-6:]
    b, i = pl.program_id(0), pl.program_id(1)
    nt = pl.num_programs(1)
    step = b * nt + i
    slot = step % 2

    def first_copy(s, bb):
        return pltpu.make_async_copy(obuf_ref.at[s, pl.ds(CH, tm - CH)],
                                     out_ref.at[bb, pl.ds(0, tm - CH)], osem.at[s])

    def tile_copy(s, bb, ii):
        return pltpu.make_async_copy(obuf_ref.at[s], out_ref.at[bb, pl.ds(ii * tm - CH, tm)], osem.at[s])

    def wait_step(s, bb, ii):
        @pl.when(ii == 0)
        def _():
            first_copy(s, bb).wait()

        @pl.when(ii > 0)
        def _():
            tile_copy(s, bb, ii).wait()

    res = _ffn_tile(tm, *refs[:-6], halo_ref, act_ref, perm_ref)

    @pl.when(step >= 2)
    def _():
        prev = step - 2
        wait_step(slot, prev // nt, prev % nt)

    obuf_ref[slot] = res

    @pl.when(i == 0)
    def _():
        first_copy(slot, b).start()

    @pl.when(i > 0)
    def _():
        tile_copy(slot, b, i).start()

    @pl.when(step == pl.num_programs(0) * nt - 1)
    def _():
        wait_step(1 - slot, (step - 1) // nt, (step - 1) % nt)
        wait_step(slot, b, i)


def _ffn(merged, h, wout, pmn, pfn, up, cw, cb, down, pon, l, tm, last):
    b, lp, d = h.shape
    assert lp // tm >= 2
    row = lambda width: pl.BlockSpec((1, tm, width), lambda i, j: (i, j, 0))
    assert tm % (2 * SUBLANES) == 0 and tm // (2 * SUBLANES) >= FFN_PARTS
    scratch = [pltpu.VMEM((FFN_PARTS * 2 * SUBLANES, 2 * D_FF), F32), pltpu.VMEM((tm, D_FF), BF16),
               pltpu.VMEM((d // LANES, tm, LANES), F32)]
    if last:
        scratch += [pltpu.VMEM((2, tm, d), F32), pltpu.SemaphoreType.DMA((2,))]
    return pl.pallas_call(
        functools.partial(_ffn_last_kernel if last else _ffn_kernel, tm),
        grid=(b, lp // tm),
        in_specs=[row(D_MIX), row(d)] + [_const_spec(a.shape) if a.ndim == 2 else _layer_spec(a, l)
                                         for a in (wout, pmn, pfn, up, cw, cb, down, pon)],
        out_specs=pl.BlockSpec(memory_space=pl.ANY) if last else row(d),
        out_shape=jax.ShapeDtypeStruct((b, lp - CH, d) if last else (b, lp, d), F32),
        scratch_shapes=scratch,
        compiler_params=pltpu.CompilerParams(
            dimension_semantics=("arbitrary", "arbitrary") if last else ("parallel", "arbitrary"),
            vmem_limit_bytes=VMEM_LIMIT),
        name="ffn_last" if last else "ffn",
    )(merged, h, wout, pmn, pfn, up, cw, cb, down, pon)


def _rope_tables(lp):
    half = RET_DK // 2
    inv = ROPE_BASE ** (-np.arange(half, dtype=np.float64) / half)
    ang = (np.arange(lp, dtype=np.float64) - PAD)[:, None] * inv[None, :]
    c, s = np.cos(ang), np.sin(ang)
    return (jnp.asarray(np.concatenate([c, c], axis=-1), F32),
            jnp.asarray(np.concatenate([-s, s], axis=-1), F32))


def _row_tile(lp):
    for tm in (640, 512, 384, 256, 128):
        if lp % tm == 0:
            return tm
    raise ValueError(f"padded length {lp} is not a multiple of {CH}")


def kernel(x, meta_tokens, pre_mix_norm, w_in, gla_gate_w2, gla_gate_b, ret_norm_w, gla_norm_w, w_out,
           post_mix_norm, pre_ffn_norm, ffn_up, ffn_conv_w, ffn_conv_b, ffn_down, post_ffn_norm):
    bsz, seq, d = x.shape
    depth = w_in.shape[0]
    assert d == D_MODEL and seq % CH == 0
    lp = CH + seq
    tm = _row_tile(lp)

    cos2, sin2 = _rope_tables(lp)
    consts = _mixer_constants(tm)

    vec = lambda v: v.reshape(depth, 1, -1).astype(F32)
    w_b = w_in.astype(BF16)
    w_ga = w_b[:, :, MAIN_W:]
    w_ga3 = jnp.concatenate(
        [w_ga, w_ga, w_ga, jnp.zeros((depth, d, LANES - 3 * GLA_GATE_RANK), BF16)], axis=2)
    w2 = gla_gate_w2.astype(F32)
    w2_hi = w2.astype(BF16)
    w2_lo = (w2 - w2_hi.astype(F32)).astype(BF16)
    w2cat = jnp.concatenate(
        [w2_hi, w2_hi, w2_lo, jnp.zeros((depth, LANES - 3 * GLA_GATE_RANK, GLA_QK), BF16)], axis=1)
    pre_mix, gate_b, ret_nw, gla_nw = vec(pre_mix_norm), vec(gla_gate_b), vec(ret_norm_w), vec(gla_norm_w)
    ffn_w = (w_out.astype(F32), ffn_up.astype(F32), ffn_down.astype(F32))
    post_mix, pre_ffn, post_ffn = vec(post_mix_norm), vec(pre_ffn_norm), vec(post_ffn_norm)
    conv_w, conv_b = ffn_conv_w.astype(F32), vec(ffn_conv_b)
    h = x.astype(F32)
    for l in range(depth):
        outs = _proj(h, meta_tokens.astype(F32) if l == 0 else None, pre_mix, w_b, w_ga3, w2cat, gate_b,
                     cos2, sin2, consts["zeta_t"], ffn_w, l, tm)
        if l == 0:
            h = outs[-1]
        wo_b, up_b, dn_b = outs[5:8]
        merged = _mixer(*outs[:5], ret_nw, gla_nw, consts, l)
        h = _ffn(merged, h, wo_b, post_mix, pre_ffn, up_b, conv_w, conv_b, dn_b, post_ffn, l, tm,
                 last=(l == depth - 1))
    return h
```

```python
import functools

import numpy as np
import jax
import jax.numpy as jnp
from jax import lax
from jax.experimental import pallas as pl
from jax.experimental.pallas import tpu as pltpu

D_MODEL = 1024
N_META = 16
RET_HEADS = 4
RET_DK = 128
RET_DV = 128
GLA_HEADS = 4
GLA_DK = 64
GLA_DV = 128
GLA_GATE_RANK = 16
GLA_TAU = 16.0
D_FF = 2816
CONV_W = 3
ROPE_BASE = 10000.0
EPS = 1e-6

RET_QK = RET_HEADS * RET_DK
RET_V = RET_HEADS * RET_DV
GLA_QK = GLA_HEADS * GLA_DK
GLA_V = GLA_HEADS * GLA_DV
D_MIX = RET_V + GLA_V
IN_WIDTH = 2 * RET_QK + 2 * RET_V + 2 * GLA_QK + 2 * GLA_V + GLA_GATE_RANK

LANES = 128
SUBLANES = 8
CH = 128
SUB_ROWS = CH // 2
PAD = CH - N_META
C_RQ, C_RK, C_RV, C_RG = 0, RET_QK, 2 * RET_QK, 2 * RET_QK + RET_V
C_GQ = C_RG + RET_V
C_GK, C_GV = C_GQ + GLA_QK, C_GQ + 2 * GLA_QK
C_GR = C_GV + GLA_V
MAIN_W = C_GR + GLA_V
MIXER_CHUNKS_PER_STEP = 5
FF_TILE = 256
FFN_PARTS = 2
VMEM_LIMIT = 56 * 1024 * 1024
LOG2E = float(np.log2(np.e))

BF16 = jnp.bfloat16
F32 = jnp.float32

_NT = (((1,), (1,)), ((), ()))
_TN = (((0,), (0,)), ((), ()))


def _rms(x):
    return x * lax.rsqrt(jnp.mean(x * x, axis=-1, keepdims=True) + EPS)


def _silu(g):
    return g * (1.0 / (1.0 + jnp.exp(-g)))


def _const_spec(shape):
    nd = len(shape)
    return pl.BlockSpec(shape, lambda *_: (0,) * nd, pipeline_mode=pl.Buffered(1))


def _layer_spec(arr, l):
    nd = arr.ndim - 1
    return pl.BlockSpec((None,) + arr.shape[1:], lambda *_: (l,) + (0,) * nd, pipeline_mode=pl.Buffered(1))


def _proj_tile(tm, h_ref, nw_ref, w_ref, wga_ref, w2_ref, gb_ref, cos_ref, sin_ref, zeta_ref,
               wo_f32_ref, up_f32_ref, dn_f32_ref,
               ret_ref, gate_ref, gqk_ref, gv_ref, la_ref, wo_b16_ref, up_b16_ref, dn_b16_ref):
    j = pl.program_id(1)
    hm = tm // 2
    parts = (slice(0, hm), slice(hm, tm))
    hn = {}

    def norm(rs):
        hn[rs.start] = (_rms(h_ref[0, rs]) * nw_ref[...]).astype(BF16)

    def cols(rs, c0, width):
        return jnp.dot(hn[rs.start], w_ref[:, c0:c0 + width], preferred_element_type=F32)

    def rope(rs, t, h):
        x = t[:, h * RET_DK:(h + 1) * RET_DK]
        return x * cos_ref[rs, :] + pltpu.roll(x, RET_DK // 2, 1) * sin_ref[rs, :]

    def put_q(rs, q):
        for h in range(RET_HEADS):
            ret_ref[0, rs, h * RET_DK:(h + 1) * RET_DK] = rope(rs, q, h).astype(BF16)

    def put_k(rs, k):
        for h in range(RET_HEADS):
            sl = slice(h * RET_DK, (h + 1) * RET_DK)
            kr = rope(rs, k, h)
            ret_ref[0, rs, RET_QK + h * RET_DK:RET_QK + (h + 1) * RET_DK] = kr.astype(BF16)
            ret_ref[0, rs, 2 * RET_QK + h * RET_DK:2 * RET_QK + (h + 1) * RET_DK] = (
                kr * zeta_ref[rs, sl]).astype(BF16)

    def put_v(rs, v):
        ret_ref[0, rs, 3 * RET_QK:] = v.astype(BF16)

    def put_rg(rs, g):
        gate_ref[0, rs, :RET_V] = _silu(g)

    def put_gr(rs, g):
        gate_ref[0, rs, RET_V:] = _silu(g)

    def put_gqk(rs, t):
        gqk_ref[0, rs, :GLA_QK] = t[:, :GLA_QK] * (GLA_DK ** -0.5)
        gqk_ref[0, rs, GLA_QK:] = t[:, GLA_QK:]

    def put_gv(rs, v):
        gv_ref[0, rs] = v.astype(BF16)

    def put_la(rs, g):
        g_hi = g.astype(BF16)
        g_lo = (g - g_hi.astype(F32)).astype(BF16)
        lane = lax.broadcasted_iota(jnp.int32, (1, LANES), 1)
        lo_part = (lane >= GLA_GATE_RANK) & (lane < 2 * GLA_GATE_RANK)
        z = jnp.dot(jnp.where(lo_part, g_lo, g_hi), w2_ref[...], preferred_element_type=F32) + gb_ref[...]
        la = (jnp.minimum(z, 0.0) - jnp.log(1.0 + jnp.exp(-jnp.abs(z)))) * (LOG2E / GLA_TAU)
        row = j * tm + rs.start + lax.broadcasted_iota(jnp.int32, (hm, 1), 0)
        la_ref[0, rs] = jnp.where(row >= PAD, la, 0.0)

    def cast_weights():
        for src, dst in ((wo_f32_ref, wo_b16_ref), (up_f32_ref, up_b16_ref), (dn_f32_ref, dn_b16_ref)):
            dst[...] = src[...].astype(BF16)

    def stages(rs):
        return [
            (lambda: jnp.dot(hn[rs.start], wga_ref[...], preferred_element_type=F32), put_la),
            (lambda: cols(rs, C_RQ, RET_QK), put_q),
            (lambda: cols(rs, C_RK, RET_QK), put_k),
            (lambda: cols(rs, C_RG, RET_V), put_rg),
            (lambda: cols(rs, C_GR, GLA_V), put_gr),
            (lambda: cols(rs, C_GQ, 2 * GLA_QK), put_gqk),
            (lambda: cols(rs, C_RV, RET_V), put_v),
            (lambda: cols(rs, C_GV, GLA_V), put_gv),
        ]

    per_part = [stages(rs) for rs in parts]
    order = [(rs, st) for group in zip(*per_part) for rs, st in zip(parts, group)]
    pending = None
    for rs, (matmul, epilogue) in order:
        if rs.start not in hn:
            norm(rs)
        res = matmul()
        if pending is not None:
            pending()
        pending = functools.partial(epilogue, rs, res)
    cast_weights()
    pending()


def _proj_kernel(tm, *refs):
    _proj_tile(tm, *refs)


def _proj_first_kernel(tm, x_ref, meta_ref, *refs):
    h_ref = refs[-1]
    j = pl.program_id(1)

    xt = x_ref[...]
    first = jnp.concatenate([jnp.zeros((PAD, D_MODEL), F32), meta_ref[...], xt[:tm - CH]], axis=0)
    h_ref[0] = jnp.where(j == 0, first, xt)
    _proj_tile(tm, h_ref, *refs[:-1])


def _x_window(tm, d):
    return pl.BlockSpec((None, pl.Element(tm), pl.Element(d)),
                        lambda i, j: (i, pl.multiple_of(jnp.maximum(j * tm - CH, 0), CH), 0))


def _slab_specs(arr, l, axis, unit, steps, nt):
    n_units = arr.shape[axis] // unit
    n_slabs = max(s for s in range(1, n_units + 1) if n_units % s == 0 and s <= steps)
    size = arr.shape[axis] // n_slabs
    shape = tuple(size if a == axis else arr.shape[a] for a in (1, 2))

    def pos(i, j):
        s = jnp.minimum(i * nt + j, n_slabs - 1)
        return (s, 0) if axis == 1 else (0, s)

    return (pl.BlockSpec((None,) + shape, lambda i, j: (l,) + pos(i, j)),
            pl.BlockSpec(shape, pos),
            jax.ShapeDtypeStruct(arr.shape[1:], BF16))


def _proj(h_or_x, meta, nw, w, wga, w2cat, gb, cos2, sin2, zeta_t, ffn_w, l, tm):
    first = meta is not None
    b, d = h_or_x.shape[0], h_or_x.shape[2]
    lp = h_or_x.shape[1] + (CH if first else 0)
    nt = lp // tm
    row = lambda width: pl.BlockSpec((1, tm, width), lambda i, j: (i, j, 0))
    tab = pl.BlockSpec((tm, RET_DK), lambda i, j: (j, 0))
    w_main = pl.BlockSpec((None, d, MAIN_W), lambda i, j: (l, 0, 0), pipeline_mode=pl.Buffered(1))
    slabs = [_slab_specs(ffn_w[0], l, 1, 4 * SUBLANES, b * nt, nt),
             _slab_specs(ffn_w[1], l, 2, 2 * LANES, b * nt, nt),
             _slab_specs(ffn_w[2], l, 1, 4 * SUBLANES, b * nt, nt)]
    lead = [_x_window(tm, d), _const_spec(meta.shape)] if first else [row(d)]
    in_specs = lead + [_layer_spec(nw, l), w_main, _layer_spec(wga, l), _layer_spec(w2cat, l), _layer_spec(gb, l),
                       tab, tab, _const_spec(zeta_t.shape)] + [s[0] for s in slabs]
    out_specs = ([row(4 * RET_QK), row(RET_V + GLA_V), row(2 * GLA_QK), row(GLA_V), row(GLA_QK)]
                 + [s[1] for s in slabs])
    out_shape = [jax.ShapeDtypeStruct((b, lp, 4 * RET_QK), BF16),
                 jax.ShapeDtypeStruct((b, lp, RET_V + GLA_V), F32),
                 jax.ShapeDtypeStruct((b, lp, 2 * GLA_QK), F32),
                 jax.ShapeDtypeStruct((b, lp, GLA_V), BF16),
                 jax.ShapeDtypeStruct((b, lp, GLA_QK), F32)] + [s[2] for s in slabs]
    if first:
        out_specs.append(row(d))
        out_shape.append(jax.ShapeDtypeStruct((b, lp, d), F32))
    args = ([h_or_x, meta] if first else [h_or_x]) + [nw, w, wga, w2cat, gb, cos2, sin2, zeta_t, *ffn_w]
    return pl.pallas_call(
        functools.partial(_proj_first_kernel if first else _proj_kernel, tm),
        grid=(b, nt),
        in_specs=in_specs,
        out_specs=out_specs,
        out_shape=out_shape,
        compiler_params=pltpu.CompilerParams(dimension_semantics=("arbitrary", "arbitrary"),
                                             vmem_limit_bytes=VMEM_LIMIT),
        name="proj_first" if first else "proj",
    )(*args)


def _level_halves():
    out, m = [], SUB_ROWS // 2
    while m >= 1:
        out.append(m)
        m //= 2
    return tuple(out)


def _mixer_constants(tm):
    c = CH
    idx = np.arange(c, dtype=np.float64)
    gam = 1.0 - 2.0 ** (-5.0 - np.arange(RET_HEADS, dtype=np.float64))
    kscale = RET_DK ** -0.5
    diff = idx[:, None] - idx[None, :]
    dmat = np.where(diff >= 0, gam[:, None, None] ** np.maximum(diff, 0.0), 0.0) * kscale
    zeta = np.repeat((gam[:, None] ** (c - 1.0 - idx)[None, :]).T, RET_DK, axis=1) * kscale
    xi = np.repeat((gam[:, None] ** (idx + 1.0)[None, :]).T, RET_DK, axis=1)
    gch = tuple(float(g ** c) for g in gam)
    s = np.arange(SUB_ROWS)
    t = np.arange(2 * SUB_ROWS) % SUB_ROWS
    ml = [(s[:, None] == t[None, :]).astype(np.float64)]
    for m in _level_halves():
        blk_s, blk_t = s // (2 * m), t // (2 * m)
        up_s, up_t = (s % (2 * m)) >= m, (t % (2 * m)) >= m
        ml.append(((blk_s[:, None] == blk_t[None, :]) & up_s[:, None] & (~up_t)[None, :]).astype(np.float64))
    mlev = np.stack(ml, axis=0)
    tril = np.tril(np.ones((c, c)))
    return dict(dmat=jnp.asarray(dmat, F32), zeta_t=jnp.asarray(np.tile(zeta, (tm // c, 1)), F32),
                xi=jnp.asarray(xi, F32), gch=gch, tril=jnp.asarray(tril, BF16), mlev=jnp.asarray(mlev, F32))


def _block_ref(cum, m):
    parts = []
    for r0 in range(0, cum.shape[0], 2 * m):
        parts.append(jnp.broadcast_to(cum[r0 + m - 1:r0 + m, :], (2 * m, cum.shape[1])))
    return parts[0] if len(parts) == 1 else jnp.concatenate(parts, axis=0)


def _mixer_kernel(gch, nb, n_sub,
                  ret_ref, gate_ref, gqk_ref, gv_ref, la_ref, rnw_ref, gnw_ref,
                  dmat_ref, xi_ref, tril_ref, mlev_ref,
                  out_ref, sret_ref, sgla_ref):
    j = pl.program_id(0)

    @pl.when(j == 0)
    def _():
        sret_ref[...] = jnp.zeros_like(sret_ref)
        sgla_ref[...] = jnp.zeros_like(sgla_ref)

    row = lax.broadcasted_iota(jnp.int32, (CH, 1), 0)
    lane = lax.broadcasted_iota(jnp.int32, (1, GLA_QK), 1)
    even = (lane % LANES) < GLA_DK
    tril = tril_ref[...]

    def retention(rs, b, h):
        sl = slice(h * RET_DK, (h + 1) * RET_DK)
        q = ret_ref[b, rs,sl]
        k = ret_ref[b, rs,RET_QK + h * RET_DK:RET_QK + (h + 1) * RET_DK]
        kz = ret_ref[b, rs,2 * RET_QK + h * RET_DK:2 * RET_QK + (h + 1) * RET_DK]
        vb = ret_ref[b, rs,3 * RET_QK + h * RET_DV:3 * RET_QK + (h + 1) * RET_DV]
        sc = lax.dot_general(q, k, _NT, preferred_element_type=F32) * dmat_ref[h]
        st = sret_ref[b, h]
        o = (jnp.dot(sc.astype(BF16), vb, preferred_element_type=F32)
             + jnp.dot(q, st.astype(BF16), preferred_element_type=F32) * xi_ref[:, sl])
        sret_ref[b, h] = st * gch[h] + lax.dot_general(kz, vb, _TN, preferred_element_type=F32)
        d = o - jnp.mean(o, axis=-1, keepdims=True)
        y = d * lax.rsqrt(jnp.mean(d * d, axis=-1, keepdims=True) + EPS) * rnw_ref[:, sl]
        out_ref[b, rs, sl] = (y * gate_ref[b, rs, sl]).astype(out_ref.dtype)

    def gla_prepare(rs, b):
        la = la_ref[b, rs]
        la_hi = la.astype(BF16)
        r1 = la - la_hi.astype(F32)
        la_mid = r1.astype(BF16)
        la_lo = (r1 - la_mid.astype(F32)).astype(BF16)
        cum = (jnp.dot(tril, la_hi, preferred_element_type=F32)
               + jnp.dot(tril, la_mid, preferred_element_type=F32)
               + jnp.dot(tril, la_lo, preferred_element_type=F32))
        qg = gqk_ref[b, rs,:GLA_QK]
        kg = gqk_ref[b, rs,GLA_QK:]
        return dict(la=la, cum=cum, qg=qg, kg_e=jnp.where(even, kg, 0.0), kg_o=jnp.where(even, 0.0, kg),
                    la_up=pltpu.roll(la, CH - 1, 0), la_dn=pltpu.roll(la, 1, 0),
                    acc={}, top={})

    levels = (0,) + _level_halves()
    subs = (slice(0, SUB_ROWS), slice(SUB_ROWS, CH))

    def head_keys(p, rows, pls, e):
        ke, ko = p["kg_e"][rows, pls], p["kg_o"][rows, pls]
        if e is not None:
            ke, ko = ke * e, ko * e
        return jnp.concatenate([ke.astype(BF16), ko.astype(BF16)], axis=0)

    def gla_level(p, pr, lv):
        pls = slice(pr * LANES, (pr + 1) * LANES)
        la, cum, qg = (p[n][:, pls] for n in ("la", "cum", "qg"))
        m = levels[lv]
        if m == 0:
            e = None
            qh = qg.astype(BF16)
        else:
            if m >= SUBLANES // 2:
                ex = -jnp.abs(cum - _block_ref(cum, m))
            elif m == 2:
                pos = row % 4
                ex = jnp.where(pos == 0, p["la_up"][:, pls],
                               jnp.where(pos == 2, la, jnp.where(pos == 3, la + p["la_dn"][:, pls], 0.0)))
            else:
                ex = jnp.where(row % 2 == 1, la, 0.0)
            e = jnp.exp2(ex)
            qh = (qg * e).astype(BF16)
        keys = jnp.concatenate([head_keys(p, rows, pls, None if e is None else e[rows]) for rows in subs], axis=0)
        sc = lax.dot_general(qh, keys, _NT, preferred_element_type=F32)
        for u, rows in enumerate(subs):
            term = sc[rows, u * 2 * SUB_ROWS:(u + 1) * 2 * SUB_ROWS] * mlev_ref[lv]
            p["acc"][pr, u] = term if lv == 0 else p["acc"][pr, u] + term

    def gla_cross(p, pr):
        pls = slice(pr * LANES, (pr + 1) * LANES)
        cum = p["cum"][:, pls]
        cref = cum[SUB_ROWS - 1:SUB_ROWS]
        qh = (p["qg"][subs[1], pls] * jnp.exp2(cum[subs[1]] - cref)).astype(BF16)
        p["top"][pr] = lax.dot_general(qh, head_keys(p, subs[0], pls, jnp.exp2(cref - cum[subs[0]])), _NT,
                                       preferred_element_type=F32)

    def gla_decays(p):
        cum = p["cum"]
        last = cum[CH - 1:CH, :]
        qd = p["qg"] * jnp.exp2(cum)
        dk = jnp.exp2(last - cum)
        p.update(qd=(jnp.where(even, qd, 0.0).astype(BF16), jnp.where(even, 0.0, qd).astype(BF16)),
                 kd=((p["kg_e"] * dk).astype(BF16), (p["kg_o"] * dk).astype(BF16)),
                 dl=jnp.exp2(last))

    def gla_finish(rs, b, p, pr):
        pls = slice(pr * LANES, (pr + 1) * LANES)
        st = sgla_ref[b, :, pls]
        stb = st.astype(BF16)
        vbs = [gv_ref[b, rs, (2 * pr + t) * GLA_DV:(2 * pr + t + 1) * GLA_DV] for t in range(2)]
        vst = [jnp.concatenate([vbs[0][rows], vbs[1][rows]], axis=0) for rows in subs]
        vst_cross = jnp.concatenate([vst[1], vst[0]], axis=0)
        for t in range(2):
            h = 2 * pr + t
            sl = slice(h * GLA_DV, (h + 1) * GLA_DV)
            osl = slice(RET_V + h * GLA_DV, RET_V + (h + 1) * GLA_DV)
            mine = even[:, :LANES] if t == 0 else ~even[:, :LANES]
            a0, a1, ax = (jnp.where(mine, a, 0.0).astype(BF16)
                          for a in (p["acc"][pr, 0], p["acc"][pr, 1], p["top"][pr]))
            o = (jnp.concatenate([jnp.dot(a0, vst[0], preferred_element_type=F32),
                                  jnp.dot(jnp.concatenate([a1, ax], axis=1), vst_cross,
                                          preferred_element_type=F32)], axis=0)
                 + lax.dot_general(p["qd"][t][:, pls], stb, _NT, preferred_element_type=F32))
            y = o * lax.rsqrt(jnp.mean(o * o, axis=-1, keepdims=True) + EPS) * gnw_ref[:, sl]
            out_ref[b, rs, osl] = (y * gate_ref[b, rs, osl]).astype(out_ref.dtype)
        kvt = lax.dot_general(jnp.concatenate(vbs, axis=0),
                              jnp.concatenate([p["kd"][0][:, pls], p["kd"][1][:, pls]], axis=0),
                              _TN, preferred_element_type=F32)
        sgla_ref[b, :, pls] = st * p["dl"][:, pls] + kvt

    fillers = []
    for c in range(n_sub):
        rs = pl.ds(c * CH, CH)
        preps = [gla_prepare(rs, b) for b in range(nb)]
        for b in range(nb):
            fillers += [functools.partial(retention, rs, b, h) for h in range(RET_HEADS)]
        for b in range(nb):
            for pr in range(GLA_HEADS // 2):
                for lv in range(len(levels)):
                    gla_level(preps[b], pr, lv)
                    if fillers:
                        fillers.pop(0)()
                gla_cross(preps[b], pr)
            gla_decays(preps[b])
            fillers += [functools.partial(gla_finish, rs, b, preps[b], pr) for pr in range(GLA_HEADS // 2)]
    for f in fillers:
        f()


def _mixer(ret, gate, gqk, gv, la, rnw, gnw, consts, l):
    b, lp, _ = gate.shape
    n_chunks = lp // CH
    n_sub = max(g for g in range(1, MIXER_CHUNKS_PER_STEP + 1) if n_chunks % g == 0)
    rows = n_sub * CH

    def blk(arr):
        return pl.BlockSpec((b, rows, arr.shape[2]), lambda j: (0, j, 0))

    tabs = [consts[k] for k in ("dmat", "xi", "tril", "mlev")]
    return pl.pallas_call(
        functools.partial(_mixer_kernel, consts["gch"], b, n_sub),
        grid=(n_chunks // n_sub,),
        in_specs=[blk(ret), blk(gate), blk(gqk), blk(gv), blk(la), _layer_spec(rnw, l), _layer_spec(gnw, l)]
        + [_const_spec(t.shape) for t in tabs],
        out_specs=pl.BlockSpec((b, rows, D_MIX), lambda j: (0, j, 0)),
        out_shape=jax.ShapeDtypeStruct((b, lp, D_MIX), BF16),
        scratch_shapes=[pltpu.VMEM((b, RET_HEADS, RET_DK, RET_DV), F32),
                        pltpu.VMEM((b, GLA_DV, GLA_QK), F32)],
        compiler_params=pltpu.CompilerParams(dimension_semantics=("arbitrary",),
                                             vmem_limit_bytes=VMEM_LIMIT),
        name="mixer",
    )(ret, gate, gqk, gv, la, rnw, gnw, *tabs)


def _gelu_tanh(x):
    return 0.5 * x * (1.0 + jnp.tanh(np.sqrt(2.0 / np.pi) * (x + 0.044715 * (x * x * x))))


def _ffn_part_bounds(tm):
    unit = 2 * SUBLANES
    n_units = tm // unit

    def split(n, k):
        if k == 1:
            return [n]
        for delta in range(n):
            for first in (n // k - delta, n // k + delta):
                if 0 < first <= n - (k - 1) and first % 4:
                    rest = split(n - first, k - 1)
                    if all(r % 4 for r in rest):
                        return [first] + rest
        return [n // k + (1 if t < n % k else 0) for t in range(k)]

    sizes = split(n_units, FFN_PARTS)
    return [sum(sizes[:t]) * unit for t in range(FFN_PARTS + 1)]


def _ffn_tile(tm, m_ref, h_ref, wout_ref, pmn_ref, pfn_ref, up_ref, cw_ref, cb_ref, down_ref, pon_ref,
              halo_ref, act_ref, perm_ref):
    i = pl.program_id(1)

    @pl.when(i == 0)
    def _():
        halo_ref[0:2 * SUBLANES] = jnp.zeros((2 * SUBLANES, 2 * D_FF), F32)

    bounds = _ffn_part_bounds(tm)
    parts = [slice(bounds[t], bounds[t + 1]) for t in range(FFN_PARTS)]
    strand = [(p.stop - p.start) // SUBLANES for p in parts]
    n_slabs = D_MODEL // LANES
    sub = lax.broadcasted_iota(jnp.int32, (SUBLANES, 1), 0)

    def to_strands(t, x):
        for k in range(n_slabs):
            perm_ref[k, parts[t], :] = x[:, k * LANES:(k + 1) * LANES]
        return jnp.concatenate(
            [jnp.concatenate([perm_ref[k, pl.ds(parts[t].start + v, SUBLANES, stride=strand[t]), :]
                              for k in range(n_slabs)], axis=1) for v in range(strand[t])], axis=0)

    def from_strands(t, val, p0):
        for kk in range(val.shape[1] // LANES):
            for v in range(strand[t]):
                perm_ref[p0 // LANES + kk, pl.ds(parts[t].start + v, SUBLANES, stride=strand[t]), :] = (
                    val[v * SUBLANES:(v + 1) * SUBLANES, kk * LANES:(kk + 1) * LANES])

    m = [jnp.dot(m_ref[0, rows], wout_ref[...], preferred_element_type=F32) for rows in parts]
    h1, hn = [], []
    for t in range(FFN_PARTS):
        x = h_ref[0, parts[t]] + _rms(m[t]) * pmn_ref[...]
        h1.append(x)
        hn.append(to_strands(t, _rms(x) * pfn_ref[...]).astype(BF16))

    def conv(t, c0):
        cs = slice(c0, c0 + FF_TILE)
        u = jnp.dot(hn[t], up_ref[:, cs], preferred_element_type=F32)
        rows = u.shape[0]
        prev2 = halo_ref[2 * SUBLANES * t:2 * SUBLANES * t + SUBLANES, cs]
        prev1 = halo_ref[2 * SUBLANES * t + SUBLANES:2 * SUBLANES * (t + 1), cs]
        nxt = 2 * SUBLANES * ((t + 1) % FFN_PARTS)
        halo_ref[nxt:nxt + 2 * SUBLANES, cs] = u[rows - 2 * SUBLANES:]
        top1 = jnp.where(sub < 1, pltpu.roll(prev1, 1, 0), pltpu.roll(u[rows - SUBLANES:], 1, 0))
        top2 = jnp.where(sub < 1, pltpu.roll(prev2, 1, 0),
                         pltpu.roll(u[rows - 2 * SUBLANES:rows - SUBLANES], 1, 0))
        s1 = jnp.concatenate([top1, u[:rows - SUBLANES]], axis=0)
        s2 = jnp.concatenate([top2, top1, u[:rows - 2 * SUBLANES]], axis=0)
        return (cb_ref[:, cs] + u * cw_ref[2:3, cs] + s1 * cw_ref[1:2, cs] + s2 * cw_ref[0:1, cs])

    def act_chunk(t, c0):
        a = conv(t, c0)
        g = conv(t, D_FF + c0)
        act_ref[parts[t], c0:c0 + FF_TILE] = (_gelu_tanh(a) * g).astype(BF16)

    def down_piece(t, p0):
        from_strands(t, jnp.dot(act_ref[parts[t], :], down_ref[:, p0:p0 + 2 * LANES],
                                preferred_element_type=F32), p0)

    chunks = list(range(0, D_FF, FF_TILE))
    for t in range(FFN_PARTS):
        pieces = list(range(0, D_MODEL, 2 * LANES)) if t > 0 else []
        for n, c0 in enumerate(chunks):
            act_chunk(t, c0)
            if pieces and n % 2 == 1:
                down_piece(t - 1, pieces.pop(0))
        for p0 in pieces:
            down_piece(t - 1, p0)
    for p0 in range(0, D_MODEL, 2 * LANES):
        down_piece(FFN_PARTS - 1, p0)
    h2 = []
    for t in range(FFN_PARTS):
        f = jnp.concatenate([perm_ref[k, parts[t], :] for k in range(n_slabs)], axis=1)
        h2.append(h1[t] + _rms(f) * pon_ref[...])
    h2 = jnp.concatenate(h2, axis=0)
    row = i * tm + lax.broadcasted_iota(jnp.int32, (tm, 1), 0)
    return jnp.where(row >= PAD, h2, 0.0)


def _ffn_kernel(tm, *refs):
    out_ref, halo_ref, act_ref, perm_ref = refs[-4:]
    out_ref[0] = _ffn_tile(tm, *refs[:-4], halo_ref, act_ref, perm_ref)


def _ffn_last_kernel(tm, *refs):
    out_ref, halo_ref, act_ref, perm_ref, obuf_ref, osem = refs[-6:]
    b, i = pl.program_id(0), pl.program_id(1)
    nt = pl.num_programs(1)
    step = b * nt + i
    slot = step % 2

    def first_copy(s, bb):
        return pltpu.make_async_copy(obuf_ref.at[s, pl.ds(CH, tm - CH)],
                                     out_ref.at[bb, pl.ds(0, tm - CH)], osem.at[s])

    def tile_copy(s, bb, ii):
        return pltpu.make_async_copy(obuf_ref.at[s], out_ref.at[bb, pl.ds(ii * tm - CH, tm)], osem.at[s])

    def wait_step(s, bb, ii):
        @pl.when(ii == 0)
        def _():
            first_copy(s, bb).wait()

        @pl.when(ii > 0)
        def _():
            tile_copy(s, bb, ii).wait()

    res = _ffn_tile(tm, *refs[:-6], halo_ref, act_ref, perm_ref)

    @pl.when(step >= 2)
    def _():
        prev = step - 2
        wait_step(slot, prev // nt, prev % nt)

    obuf_ref[slot] = res

    @pl.when(i == 0)
    def _():
        first_copy(slot, b).start()

    @pl.when(i > 0)
    def _():
        tile_copy(slot, b, i).start()

    @pl.when(step == pl.num_programs(0) * nt - 1)
    def _():
        wait_step(1 - slot, (step - 1) // nt, (step - 1) % nt)
        wait_step(slot, b, i)


def _ffn(merged, h, wout, pmn, pfn, up, cw, cb, down, pon, l, tm, last):
    b, lp, d = h.shape
    assert lp // tm >= 2
    row = lambda width: pl.BlockSpec((1, tm, width), lambda i, j: (i, j, 0))
    assert tm % (2 * SUBLANES) == 0 and tm // (2 * SUBLANES) >= FFN_PARTS
    scratch = [pltpu.VMEM((FFN_PARTS * 2 * SUBLANES, 2 * D_FF), F32), pltpu.VMEM((tm, D_FF), BF16),
               pltpu.VMEM((d // LANES, tm, LANES), F32)]
    if last:
        scratch += [pltpu.VMEM((2, tm, d), F32), pltpu.SemaphoreType.DMA((2,))]
    return pl.pallas_call(
        functools.partial(_ffn_last_kernel if last else _ffn_kernel, tm),
        grid=(b, lp // tm),
        in_specs=[row(D_MIX), row(d)] + [_const_spec(a.shape) if a.ndim == 2 else _layer_spec(a, l)
                                         for a in (wout, pmn, pfn, up, cw, cb, down, pon)],
        out_specs=pl.BlockSpec(memory_space=pl.ANY) if last else row(d),
        out_shape=jax.ShapeDtypeStruct((b, lp - CH, d) if last else (b, lp, d), F32),
        scratch_shapes=scratch,
        compiler_params=pltpu.CompilerParams(
            dimension_semantics=("arbitrary", "arbitrary") if last else ("parallel", "arbitrary"),
            vmem_limit_bytes=VMEM_LIMIT),
        name="ffn_last" if last else "ffn",
    )(merged, h, wout, pmn, pfn, up, cw, cb, down, pon)


def _rope_tables(lp):
    half = RET_DK // 2
    inv = ROPE_BASE ** (-np.arange(half, dtype=np.float64) / half)
    ang = (np.arange(lp, dtype=np.float64) - PAD)[:, None] * inv[None, :]
    c, s = np.cos(ang), np.sin(ang)
    return (jnp.asarray(np.concatenate([c, c], axis=-1), F32),
            jnp.asarray(np.concatenate([-s, s], axis=-1), F32))


def _row_tile(lp):
    for tm in (640, 512, 384, 256, 128):
        if lp % tm == 0:
            return tm
    raise ValueError(f"padded length {lp} is not a multiple of {CH}")


def kernel(x, meta_tokens, pre_mix_norm, w_in, gla_gate_w2, gla_gate_b, ret_norm_w, gla_norm_w, w_out,
           post_mix_norm, pre_ffn_norm, ffn_up, ffn_conv_w, ffn_conv_b, ffn_down, post_ffn_norm):
    bsz, seq, d = x.shape
    depth = w_in.shape[0]
    assert d == D_MODEL and seq % CH == 0
    lp = CH + seq
    tm = _row_tile(lp)

    cos2, sin2 = _rope_tables(lp)
    consts = _mixer_constants(tm)

    vec = lambda v: v.reshape(depth, 1, -1).astype(F32)
    w_b = w_in.astype(BF16)
    w_ga = w_b[:, :, MAIN_W:]
    w_ga3 = jnp.concatenate(
        [w_ga, w_ga, w_ga, jnp.zeros((depth, d, LANES - 3 * GLA_GATE_RANK), BF16)], axis=2)
    w2 = gla_gate_w2.astype(F32)
    w2_hi = w2.astype(BF16)
    w2_lo = (w2 - w2_hi.astype(F32)).astype(BF16)
    w2cat = jnp.concatenate(
        [w2_hi, w2_hi, w2_lo, jnp.zeros((depth, LANES - 3 * GLA_GATE_RANK, GLA_QK), BF16)], axis=1)
    pre_mix, gate_b, ret_nw, gla_nw = vec(pre_mix_norm), vec(gla_gate_b), vec(ret_norm_w), vec(gla_norm_w)
    ffn_w = (w_out.astype(F32), ffn_up.astype(F32), ffn_down.astype(F32))
    post_mix, pre_ffn, post_ffn = vec(post_mix_norm), vec(pre_ffn_norm), vec(post_ffn_norm)
    conv_w, conv_b = ffn_conv_w.astype(F32), vec(ffn_conv_b)
    h = x.astype(F32)
    for l in range(depth):
        outs = _proj(h, meta_tokens.astype(F32) if l == 0 else None, pre_mix, w_b, w_ga3, w2cat, gate_b,
                     cos2, sin2, consts["zeta_t"], ffn_w, l, tm)
        if l == 0:
            h = outs[-1]
        wo_b, up_b, dn_b = outs[5:8]
        merged = _mixer(*outs[:5], ret_nw, gla_nw, consts, l)
        h = _ffn(merged, h, wo_b, post_mix, pre_ffn, up_b, conv_w, conv_b, dn_b, post_ffn, l, tm,
                 last=(l == depth - 1))
    return h
```

```python
import functools

import numpy as np
import jax
import jax.numpy as jnp
from jax import lax
from jax.experimental import pallas as pl
from jax.experimental.pallas import tpu as pltpu

D_MODEL = 1024
N_META = 16
RET_HEADS = 4
RET_DK = 128
RET_DV = 128
GLA_HEADS = 4
GLA_DK = 64
GLA_DV = 128
GLA_GATE_RANK = 16
GLA_TAU = 16.0
D_FF = 2816
CONV_W = 3
ROPE_BASE = 10000.0
EPS = 1e-6

RET_QK = RET_HEADS * RET_DK
RET_V = RET_HEADS * RET_DV
GLA_QK = GLA_HEADS * GLA_DK
GLA_V = GLA_HEADS * GLA_DV
D_MIX = RET_V + GLA_V
IN_WIDTH = 2 * RET_QK + 2 * RET_V + 2 * GLA_QK + 2 * GLA_V + GLA_GATE_RANK

LANES = 128
SUBLANES = 8
CH = 128
SUB_ROWS = CH // 2
PAD = CH - N_META
C_RQ, C_RK, C_RV, C_RG = 0, RET_QK, 2 * RET_QK, 2 * RET_QK + RET_V
C_GQ = C_RG + RET_V
C_GK, C_GV = C_GQ + GLA_QK, C_GQ + 2 * GLA_QK
C_GR = C_GV + GLA_V
MAIN_W = C_GR + GLA_V
MIXER_CHUNKS_PER_STEP = 5
FF_TILE = 256
FFN_PARTS = 2
VMEM_LIMIT = 56 * 1024 * 1024
LOG2E = float(np.log2(np.e))

BF16 = jnp.bfloat16
F32 = jnp.float32

_NT = (((1,), (1,)), ((), ()))
_TN = (((0,), (0,)), ((), ()))


def _rms(x):
    return x * lax.rsqrt(jnp.mean(x * x, axis=-1, keepdims=True) + EPS)


def _silu(g):
    return g * (1.0 / (1.0 + jnp.exp(-g)))


def _const_spec(shape):
    nd = len(shape)
    return pl.BlockSpec(shape, lambda *_: (0,) * nd, pipeline_mode=pl.Buffered(1))


def _layer_spec(arr, l):
    nd = arr.ndim - 1
    return pl.BlockSpec((None,) + arr.shape[1:], lambda *_: (l,) + (0,) * nd, pipeline_mode=pl.Buffered(1))


def _proj_tile(tm, h_ref, nw_ref, w_ref, wga_ref, w2_ref, gb_ref, cos_ref, sin_ref, zeta_ref,
               wo_f32_ref, up_f32_ref, dn_f32_ref,
               ret_ref, gate_ref, gqk_ref, gv_ref, la_ref, wo_b16_ref, up_b16_ref, dn_b16_ref):
    j = pl.program_id(1)
    hm = tm // 2
    parts = (slice(0, hm), slice(hm, tm))
    hn = {}

    def norm(rs):
        hn[rs.start] = (_rms(h_ref[0, rs]) * nw_ref[...]).astype(BF16)

    def cols(rs, c0, width):
        return jnp.dot(hn[rs.start], w_ref[:, c0:c0 + width], preferred_element_type=F32)

    def rope(rs, t, h):
        x = t[:, h * RET_DK:(h + 1) * RET_DK]
        return x * cos_ref[rs, :] + pltpu.roll(x, RET_DK // 2, 1) * sin_ref[rs, :]

    def put_q(rs, q):
        for h in range(RET_HEADS):
            ret_ref[0, rs, h * RET_DK:(h + 1) * RET_DK] = rope(rs, q, h).astype(BF16)

    def put_k(rs, k):
        for h in range(RET_HEADS):
            sl = slice(h * RET_DK, (h + 1) * RET_DK)
            kr = rope(rs, k, h)
            ret_ref[0, rs, RET_QK + h * RET_DK:RET_QK + (h + 1) * RET_DK] = kr.astype(BF16)
            ret_ref[0, rs, 2 * RET_QK + h * RET_DK:2 * RET_QK + (h + 1) * RET_DK] = (
                kr * zeta_ref[rs, sl]).astype(BF16)

    def put_v(rs, v):
        ret_ref[0, rs, 3 * RET_QK:] = v.astype(BF16)

    def put_rg(rs, g):
        gate_ref[0, rs, :RET_V] = _silu(g)

    def put_gr(rs, g):
        gate_ref[0, rs, RET_V:] = _silu(g)

    def put_gqk(rs, t):
        gqk_ref[0, rs, :GLA_QK] = t[:, :GLA_QK] * (GLA_DK ** -0.5)
        gqk_ref[0, rs, GLA_QK:] = t[:, GLA_QK:]

    def put_gv(rs, v):
        gv_ref[0, rs] = v.astype(BF16)

    def put_la(rs, g):
        g_hi = g.astype(BF16)
        g_lo = (g - g_hi.astype(F32)).astype(BF16)
        lane = lax.broadcasted_iota(jnp.int32, (1, LANES), 1)
        lo_part = (lane >= GLA_GATE_RANK) & (lane < 2 * GLA_GATE_RANK)
        z = jnp.dot(jnp.where(lo_part, g_lo, g_hi), w2_ref[...], preferred_element_type=F32) + gb_ref[...]
        la = (jnp.minimum(z, 0.0) - jnp.log(1.0 + jnp.exp(-jnp.abs(z)))) * (LOG2E / GLA_TAU)
        row = j * tm + rs.start + lax.broadcasted_iota(jnp.int32, (hm, 1), 0)
        la_ref[0, rs] = jnp.where(row >= PAD, la, 0.0)

    def cast_weights():
        for src, dst in ((wo_f32_ref, wo_b16_ref), (up_f32_ref, up_b16_ref), (dn_f32_ref, dn_b16_ref)):
            dst[...] = src[...].astype(BF16)

    def stages(rs):
        return [
            (lambda: jnp.dot(hn[rs.start], wga_ref[...], preferred_element_type=F32), put_la),
            (lambda: cols(rs, C_RQ, RET_QK), put_q),
            (lambda: cols(rs, C_RK, RET_QK), put_k),
            (lambda: cols(rs, C_RG, RET_V), put_rg),
            (lambda: cols(rs, C_GR, GLA_V), put_gr),
            (lambda: cols(rs, C_GQ, 2 * GLA_QK), put_gqk),
            (lambda: cols(rs, C_RV, RET_V), put_v),
            (lambda: cols(rs, C_GV, GLA_V), put_gv),
        ]

    per_part = [stages(rs) for rs in parts]
    order = [(rs, st) for group in zip(*per_part) for rs, st in zip(parts, group)]
    pending = None
    for rs, (matmul, epilogue) in order:
        if rs.start not in hn:
            norm(rs)
        res = matmul()
        if pending is not None:
            pending()
        pending = functools.partial(epilogue, rs, res)
    cast_weights()
    pending()


def _proj_kernel(tm, *refs):
    _proj_tile(tm, *refs)


def _proj_first_kernel(tm, x_ref, meta_ref, *refs):
    h_ref = refs[-1]
    j = pl.program_id(1)

    xt = x_ref[...]
    first = jnp.concatenate([jnp.zeros((PAD, D_MODEL), F32), meta_ref[...], xt[:tm - CH]], axis=0)
    h_ref[0] = jnp.where(j == 0, first, xt)
    _proj_tile(tm, h_ref, *refs[:-1])


def _x_window(tm, d):
    return pl.BlockSpec((None, pl.Element(tm), pl.Element(d)),
                        lambda i, j: (i, pl.multiple_of(jnp.maximum(j * tm - CH, 0), CH), 0))


def _slab_specs(arr, l, axis, unit, steps, nt):
    n_units = arr.shape[axis] // unit
    n_slabs = max(s for s in range(1, n_units + 1) if n_units % s == 0 and s <= steps)
    size = arr.shape[axis] // n_slabs
    shape = tuple(size if a == axis else arr.shape[a] for a in (1, 2))

    def pos(i, j):
        s = jnp.minimum(i * nt + j, n_slabs - 1)
        return (s, 0) if axis == 1 else (0, s)

    return (pl.BlockSpec((None,) + shape, lambda i, j: (l,) + pos(i, j)),
            pl.BlockSpec(shape, pos),
            jax.ShapeDtypeStruct(arr.shape[1:], BF16))


def _proj(h_or_x, meta, nw, w, wga, w2cat, gb, cos2, sin2, zeta_t, ffn_w, l, tm):
    first = meta is not None
    b, d = h_or_x.shape[0], h_or_x.shape[2]
    lp = h_or_x.shape[1] + (CH if first else 0)
    nt = lp // tm
    row = lambda width: pl.BlockSpec((1, tm, width), lambda i, j: (i, j, 0))
    tab = pl.BlockSpec((tm, RET_DK), lambda i, j: (j, 0))
    w_main = pl.BlockSpec((None, d, MAIN_W), lambda i, j: (l, 0, 0), pipeline_mode=pl.Buffered(1))
    slabs = [_slab_specs(ffn_w[0], l, 1, 4 * SUBLANES, b * nt, nt),
             _slab_specs(ffn_w[1], l, 2, 2 * LANES, b * nt, nt),
             _slab_specs(ffn_w[2], l, 1, 4 * SUBLANES, b * nt, nt)]
    lead = [_x_window(tm, d), _const_spec(meta.shape)] if first else [row(d)]
    in_specs = lead + [_layer_spec(nw, l), w_main, _layer_spec(wga, l), _layer_spec(w2cat, l), _layer_spec(gb, l),
                       tab, tab, _const_spec(zeta_t.shape)] + [s[0] for s in slabs]
    out_specs = ([row(4 * RET_QK), row(RET_V + GLA_V), row(2 * GLA_QK), row(GLA_V), row(GLA_QK)]
                 + [s[1] for s in slabs])
    out_shape = [jax.ShapeDtypeStruct((b, lp, 4 * RET_QK), BF16),
                 jax.ShapeDtypeStruct((b, lp, RET_V + GLA_V), F32),
                 jax.ShapeDtypeStruct((b, lp, 2 * GLA_QK), F32),
                 jax.ShapeDtypeStruct((b, lp, GLA_V), BF16),
                 jax.ShapeDtypeStruct((b, lp, GLA_QK), F32)] + [s[2] for s in slabs]
    if first:
        out_specs.append(row(d))
        out_shape.append(jax.ShapeDtypeStruct((b, lp, d), F32))
    args = ([h_or_x, meta] if first else [h_or_x]) + [nw, w, wga, w2cat, gb, cos2, sin2, zeta_t, *ffn_w]
    return pl.pallas_call(
        functools.partial(_proj_first_kernel if first else _proj_kernel, tm),
        grid=(b, nt),
        in_specs=in_specs,
        out_specs=out_specs,
        out_shape=out_shape,
        compiler_params=pltpu.CompilerParams(dimension_semantics=("arbitrary", "arbitrary"),
                                             vmem_limit_bytes=VMEM_LIMIT),
        name="proj_first" if first else "proj",
    )(*args)


def _level_halves():
    out, m = [], SUB_ROWS // 2
    while m >= 1:
        out.append(m)
        m //= 2
    return tuple(out)


def _mixer_constants(tm):
    c = CH
    idx = np.arange(c, dtype=np.float64)
    gam = 1.0 - 2.0 ** (-5.0 - np.arange(RET_HEADS, dtype=np.float64))
    kscale = RET_DK ** -0.5
    diff = idx[:, None] - idx[None, :]
    dmat = np.where(diff >= 0, gam[:, None, None] ** np.maximum(diff, 0.0), 0.0) * kscale
    zeta = np.repeat((gam[:, None] ** (c - 1.0 - idx)[None, :]).T, RET_DK, axis=1) * kscale
    xi = np.repeat((gam[:, None] ** (idx + 1.0)[None, :]).T, RET_DK, axis=1)
    gch = tuple(float(g ** c) for g in gam)
    s = np.arange(SUB_ROWS)
    t = np.arange(2 * SUB_ROWS) % SUB_ROWS
    ml = [(s[:, None] == t[None, :]).astype(np.float64)]
    for m in _level_halves():
        blk_s, blk_t = s // (2 * m), t // (2 * m)
        up_s, up_t = (s % (2 * m)) >= m, (t % (2 * m)) >= m
        ml.append(((blk_s[:, None] == blk_t[None, :]) & up_s[:, None] & (~up_t)[None, :]).astype(np.float64))
    mlev = np.stack(ml, axis=0)
    tril = np.tril(np.ones((c, c)))
    return dict(dmat=jnp.asarray(dmat, F32), zeta_t=jnp.asarray(np.tile(zeta, (tm // c, 1)), F32),
                xi=jnp.asarray(xi, F32), gch=gch, tril=jnp.asarray(tril, BF16), mlev=jnp.asarray(mlev, F32))


def _block_ref(cum, m):
    parts = []
    for r0 in range(0, cum.shape[0], 2 * m):
        parts.append(jnp.broadcast_to(cum[r0 + m - 1:r0 + m, :], (2 * m, cum.shape[1])))
    return parts[0] if len(parts) == 1 else jnp.concatenate(parts, axis=0)


def _mixer_kernel(gch, nb, n_sub,
                  ret_ref, gate_ref, gqk_ref, gv_ref, la_ref, rnw_ref, gnw_ref,
                  dmat_ref, xi_ref, tril_ref, mlev_ref,
                  out_ref, sret_ref, sgla_ref):
    j = pl.program_id(0)

    sret_ref[...] = jnp.where(j == 0, 0.0, sret_ref[...])
    sgla_ref[...] = jnp.where(j == 0, 0.0, sgla_ref[...])

    row = lax.broadcasted_iota(jnp.int32, (CH, 1), 0)
    lane = lax.broadcasted_iota(jnp.int32, (1, GLA_QK), 1)
    even = (lane % LANES) < GLA_DK
    tril = tril_ref[...]

    def retention(rs, b, h):
        sl = slice(h * RET_DK, (h + 1) * RET_DK)
        q = ret_ref[b, rs,sl]
        k = ret_ref[b, rs,RET_QK + h * RET_DK:RET_QK + (h + 1) * RET_DK]
        kz = ret_ref[b, rs,2 * RET_QK + h * RET_DK:2 * RET_QK + (h + 1) * RET_DK]
        vb = ret_ref[b, rs,3 * RET_QK + h * RET_DV:3 * RET_QK + (h + 1) * RET_DV]
        sc = lax.dot_general(q, k, _NT, preferred_element_type=F32) * dmat_ref[h]
        st = sret_ref[b, h]
        o = (jnp.dot(sc.astype(BF16), vb, preferred_element_type=F32)
             + jnp.dot(q, st.astype(BF16), preferred_element_type=F32) * xi_ref[:, sl])
        sret_ref[b, h] = st * gch[h] + lax.dot_general(kz, vb, _TN, preferred_element_type=F32)
        d = o - jnp.mean(o, axis=-1, keepdims=True)
        y = d * lax.rsqrt(jnp.mean(d * d, axis=-1, keepdims=True) + EPS) * rnw_ref[:, sl]
        out_ref[b, rs, sl] = (y * gate_ref[b, rs, sl]).astype(out_ref.dtype)

    def gla_prepare(rs, b):
        la = la_ref[b, rs]
        la_hi = la.astype(BF16)
        r1 = la - la_hi.astype(F32)
        la_mid = r1.astype(BF16)
        la_lo = (r1 - la_mid.astype(F32)).astype(BF16)
        cum = (jnp.dot(tril, la_hi, preferred_element_type=F32)
               + jnp.dot(tril, la_mid, preferred_element_type=F32)
               + jnp.dot(tril, la_lo, preferred_element_type=F32))
        qg = gqk_ref[b, rs,:GLA_QK]
        kg = gqk_ref[b, rs,GLA_QK:]
        return dict(la=la, cum=cum, qg=qg, kg_e=jnp.where(even, kg, 0.0), kg_o=jnp.where(even, 0.0, kg),
                    la_up=pltpu.roll(la, CH - 1, 0), la_dn=pltpu.roll(la, 1, 0),
                    acc={}, top={})

    levels = (0,) + _level_halves()
    subs = (slice(0, SUB_ROWS), slice(SUB_ROWS, CH))

    def head_keys(p, rows, pls, e):
        ke, ko = p["kg_e"][rows, pls], p["kg_o"][rows, pls]
        if e is not None:
            ke, ko = ke * e, ko * e
        return jnp.concatenate([ke.astype(BF16), ko.astype(BF16)], axis=0)

    def gla_level(p, pr, lv):
        pls = slice(pr * LANES, (pr + 1) * LANES)
        la, cum, qg = (p[n][:, pls] for n in ("la", "cum", "qg"))
        m = levels[lv]
        if m == 0:
            e = None
            qh = qg.astype(BF16)
        else:
            if m >= SUBLANES // 2:
                ex = -jnp.abs(cum - _block_ref(cum, m))
            elif m == 2:
                pos = row % 4
                ex = jnp.where(pos == 0, p["la_up"][:, pls],
                               jnp.where(pos == 2, la, jnp.where(pos == 3, la + p["la_dn"][:, pls], 0.0)))
            else:
                ex = jnp.where(row % 2 == 1, la, 0.0)
            e = jnp.exp2(ex)
            qh = (qg * e).astype(BF16)
        keys = jnp.concatenate([head_keys(p, rows, pls, None if e is None else e[rows]) for rows in subs], axis=0)
        sc = lax.dot_general(qh, keys, _NT, preferred_element_type=F32)
        for u, rows in enumerate(subs):
            term = sc[rows, u * 2 * SUB_ROWS:(u + 1) * 2 * SUB_ROWS] * mlev_ref[lv]
            p["acc"][pr, u] = term if lv == 0 else p["acc"][pr, u] + term

    def gla_cross(p, pr):
        pls = slice(pr * LANES, (pr + 1) * LANES)
        cum = p["cum"][:, pls]
        cref = cum[SUB_ROWS - 1:SUB_ROWS]
        qh = (p["qg"][subs[1], pls] * jnp.exp2(cum[subs[1]] - cref)).astype(BF16)
        p["top"][pr] = lax.dot_general(qh, head_keys(p, subs[0], pls, jnp.exp2(cref - cum[subs[0]])), _NT,
                                       preferred_element_type=F32)

    def gla_decays(p):
        cum = p["cum"]
        last = cum[CH - 1:CH, :]
        qd = p["qg"] * jnp.exp2(cum)
        dk = jnp.exp2(last - cum)
        p.update(qd=(jnp.where(even, qd, 0.0).astype(BF16), jnp.where(even, 0.0, qd).astype(BF16)),
                 kd=((p["kg_e"] * dk).astype(BF16), (p["kg_o"] * dk).astype(BF16)),
                 dl=jnp.exp2(last))

    def gla_finish(rs, b, p, pr):
        pls = slice(pr * LANES, (pr + 1) * LANES)
        st = sgla_ref[b, :, pls]
        stb = st.astype(BF16)
        vbs = [gv_ref[b, rs, (2 * pr + t) * GLA_DV:(2 * pr + t + 1) * GLA_DV] for t in range(2)]
        vst = [jnp.concatenate([vbs[0][rows], vbs[1][rows]], axis=0) for rows in subs]
        vst_cross = jnp.concatenate([vst[1], vst[0]], axis=0)
        for t in range(2):
            h = 2 * pr + t
            sl = slice(h * GLA_DV, (h + 1) * GLA_DV)
            osl = slice(RET_V + h * GLA_DV, RET_V + (h + 1) * GLA_DV)
            mine = even[:, :LANES] if t == 0 else ~even[:, :LANES]
            a0, a1, ax = (jnp.where(mine, a, 0.0).astype(BF16)
                          for a in (p["acc"][pr, 0], p["acc"][pr, 1], p["top"][pr]))
            o = (jnp.concatenate([jnp.dot(a0, vst[0], preferred_element_type=F32),
                                  jnp.dot(jnp.concatenate([a1, ax], axis=1), vst_cross,
                                          preferred_element_type=F32)], axis=0)
                 + lax.dot_general(p["qd"][t][:, pls], stb, _NT, preferred_element_type=F32))
            y = o * lax.rsqrt(jnp.mean(o * o, axis=-1, keepdims=True) + EPS) * gnw_ref[:, sl]
            out_ref[b, rs, osl] = (y * gate_ref[b, rs, osl]).astype(out_ref.dtype)
        kvt = lax.dot_general(jnp.concatenate(vbs, axis=0),
                              jnp.concatenate([p["kd"][0][:, pls], p["kd"][1][:, pls]], axis=0),
                              _TN, preferred_element_type=F32)
        sgla_ref[b, :, pls] = st * p["dl"][:, pls] + kvt

    fillers = []
    for c in range(n_sub):
        rs = pl.ds(c * CH, CH)
        preps = [gla_prepare(rs, b) for b in range(nb)]
        for b in range(nb):
            fillers += [functools.partial(retention, rs, b, h) for h in range(RET_HEADS)]
        for b in range(nb):
            for pr in range(GLA_HEADS // 2):
                for lv in range(len(levels)):
                    gla_level(preps[b], pr, lv)
                    if fillers:
                        fillers.pop(0)()
                gla_cross(preps[b], pr)
            gla_decays(preps[b])
            fillers += [functools.partial(gla_finish, rs, b, preps[b], pr) for pr in range(GLA_HEADS // 2)]
    for f in fillers:
        f()


def _mixer(ret, gate, gqk, gv, la, rnw, gnw, consts, l):
    b, lp, _ = gate.shape
    n_chunks = lp // CH
    n_sub = max(g for g in range(1, MIXER_CHUNKS_PER_STEP + 1) if n_chunks % g == 0)
    rows = n_sub * CH

    def blk(arr):
        return pl.BlockSpec((b, rows, arr.shape[2]), lambda j: (0, j, 0))

    tabs = [consts[k] for k in ("dmat", "xi", "tril", "mlev")]
    return pl.pallas_call(
        functools.partial(_mixer_kernel, consts["gch"], b, n_sub),
        grid=(n_chunks // n_sub,),
        in_specs=[blk(ret), blk(gate), blk(gqk), blk(gv), blk(la), _layer_spec(rnw, l), _layer_spec(gnw, l)]
        + [_const_spec(t.shape) for t in tabs],
        out_specs=pl.BlockSpec((b, rows, D_MIX), lambda j: (0, j, 0)),
        out_shape=jax.ShapeDtypeStruct((b, lp, D_MIX), BF16),
        scratch_shapes=[pltpu.VMEM((b, RET_HEADS, RET_DK, RET_DV), F32),
                        pltpu.VMEM((b, GLA_DV, GLA_QK), F32)],
        compiler_params=pltpu.CompilerParams(dimension_semantics=("arbitrary",),
                                             vmem_limit_bytes=VMEM_LIMIT),
        name="mixer",
    )(ret, gate, gqk, gv, la, rnw, gnw, *tabs)


def _gelu_tanh(x):
    return 0.5 * x * (1.0 + jnp.tanh(np.sqrt(2.0 / np.pi) * (x + 0.044715 * (x * x * x))))


def _ffn_part_bounds(tm):
    unit = 2 * SUBLANES
    n_units = tm // unit

    def split(n, k):
        if k == 1:
            return [n]
        for delta in range(n):
            for first in (n // k - delta, n // k + delta):
                if 0 < first <= n - (k - 1) and first % 4:
                    rest = split(n - first, k - 1)
                    if all(r % 4 for r in rest):
                        return [first] + rest
        return [n // k + (1 if t < n % k else 0) for t in range(k)]

    sizes = split(n_units, FFN_PARTS)
    return [sum(sizes[:t]) * unit for t in range(FFN_PARTS + 1)]


def _ffn_tile(tm, m_ref, h_ref, wout_ref, pmn_ref, pfn_ref, up_ref, cw_ref, cb_ref, down_ref, pon_ref,
              halo_ref, act_ref, perm_ref):
    i = pl.program_id(1)

    halo_ref[0:2 * SUBLANES] = jnp.where(i == 0, 0.0, halo_ref[0:2 * SUBLANES])

    bounds = _ffn_part_bounds(tm)
    parts = [slice(bounds[t], bounds[t + 1]) for t in range(FFN_PARTS)]
    strand = [(p.stop - p.start) // SUBLANES for p in parts]
    n_slabs = D_MODEL // LANES
    sub = lax.broadcasted_iota(jnp.int32, (SUBLANES, 1), 0)

    def to_strands(t, x):
        for k in range(n_slabs):
            perm_ref[k, parts[t], :] = x[:, k * LANES:(k + 1) * LANES]
        return jnp.concatenate(
            [jnp.concatenate([perm_ref[k, pl.ds(parts[t].start + v, SUBLANES, stride=strand[t]), :]
                              for k in range(n_slabs)], axis=1) for v in range(strand[t])], axis=0)

    def from_strands(t, val, p0):
        for kk in range(val.shape[1] // LANES):
            for v in range(strand[t]):
                perm_ref[p0 // LANES + kk, pl.ds(parts[t].start + v, SUBLANES, stride=strand[t]), :] = (
                    val[v * SUBLANES:(v + 1) * SUBLANES, kk * LANES:(kk + 1) * LANES])

    m = [jnp.dot(m_ref[0, rows], wout_ref[...], preferred_element_type=F32) for rows in parts]
    h1, hn = [], []
    for t in range(FFN_PARTS):
        x = h_ref[0, parts[t]] + _rms(m[t]) * pmn_ref[...]
        h1.append(x)
        hn.append(to_strands(t, _rms(x) * pfn_ref[...]).astype(BF16))

    def conv(t, c0):
        cs = slice(c0, c0 + FF_TILE)
        u = jnp.dot(hn[t], up_ref[:, cs], preferred_element_type=F32)
        rows = u.shape[0]
        prev2 = halo_ref[2 * SUBLANES * t:2 * SUBLANES * t + SUBLANES, cs]
        prev1 = halo_ref[2 * SUBLANES * t + SUBLANES:2 * SUBLANES * (t + 1), cs]
        nxt = 2 * SUBLANES * ((t + 1) % FFN_PARTS)
        halo_ref[nxt:nxt + 2 * SUBLANES, cs] = u[rows - 2 * SUBLANES:]
        top1 = jnp.where(sub < 1, pltpu.roll(prev1, 1, 0), pltpu.roll(u[rows - SUBLANES:], 1, 0))
        top2 = jnp.where(sub < 1, pltpu.roll(prev2, 1, 0),
                         pltpu.roll(u[rows - 2 * SUBLANES:rows - SUBLANES], 1, 0))
        s1 = jnp.concatenate([top1, u[:rows - SUBLANES]], axis=0)
        s2 = jnp.concatenate([top2, top1, u[:rows - 2 * SUBLANES]], axis=0)
        return (cb_ref[:, cs] + u * cw_ref[2:3, cs] + s1 * cw_ref[1:2, cs] + s2 * cw_ref[0:1, cs])

    def act_chunk(t, c0):
        a = conv(t, c0)
        g = conv(t, D_FF + c0)
        act_ref[parts[t], c0:c0 + FF_TILE] = (_gelu_tanh(a) * g).astype(BF16)

    def down_piece(t, p0):
        from_strands(t, jnp.dot(act_ref[parts[t], :], down_ref[:, p0:p0 + 2 * LANES],
                                preferred_element_type=F32), p0)

    chunks = list(range(0, D_FF, FF_TILE))
    for t in range(FFN_PARTS):
        pieces = list(range(0, D_MODEL, 2 * LANES)) if t > 0 else []
        for n, c0 in enumerate(chunks):
            act_chunk(t, c0)
            if pieces and n % 2 == 1:
                down_piece(t - 1, pieces.pop(0))
        for p0 in pieces:
            down_piece(t - 1, p0)
    for p0 in range(0, D_MODEL, 2 * LANES):
        down_piece(FFN_PARTS - 1, p0)
    h2 = []
    for t in range(FFN_PARTS):
        f = jnp.concatenate([perm_ref[k, parts[t], :] for k in range(n_slabs)], axis=1)
        h2.append(h1[t] + _rms(f) * pon_ref[...])
    h2 = jnp.concatenate(h2, axis=0)
    row = i * tm + lax.broadcasted_iota(jnp.int32, (tm, 1), 0)
    return jnp.where(row >= PAD, h2, 0.0)


def _ffn_kernel(tm, *refs):
    out_ref, halo_ref, act_ref, perm_ref = refs[-4:]
    out_ref[0] = _ffn_tile(tm, *refs[:-4], halo_ref, act_ref, perm_ref)


def _ffn_last_kernel(tm, *refs):
    out_ref, halo_ref, act_ref, perm_ref, obuf_ref, osem = refs[-6:]
    b, i = pl.program_id(0), pl.program_id(1)
    nt = pl.num_programs(1)
    step = b * nt + i
    slot = step % 2

    def first_copy(s, bb):
        return pltpu.make_async_copy(obuf_ref.at[s, pl.ds(CH, tm - CH)],
                                     out_ref.at[bb, pl.ds(0, tm - CH)], osem.at[s])

    def tile_copy(s, bb, ii):
        return pltpu.make_async_copy(obuf_ref.at[s], out_ref.at[bb, pl.ds(ii * tm - CH, tm)], osem.at[s])

    def wait_step(s, bb, ii):
        @pl.when(ii == 0)
        def _():
            first_copy(s, bb).wait()

        @pl.when(ii > 0)
        def _():
            tile_copy(s, bb, ii).wait()

    res = _ffn_tile(tm, *refs[:-6], halo_ref, act_ref, perm_ref)

    @pl.when(step >= 2)
    def _():
        prev = step - 2
        wait_step(slot, prev // nt, prev % nt)

    obuf_ref[slot] = res

    @pl.when(i == 0)
    def _():
        first_copy(slot, b).start()

    @pl.when(i > 0)
    def _():
        tile_copy(slot, b, i).start()

    @pl.when(step == pl.num_programs(0) * nt - 1)
    def _():
        wait_step(1 - slot, (step - 1) // nt, (step - 1) % nt)
        wait_step(slot, b, i)


def _ffn(merged, h, wout, pmn, pfn, up, cw, cb, down, pon, l, tm, last):
    b, lp, d = h.shape
    assert lp // tm >= 2
    row = lambda width: pl.BlockSpec((1, tm, width), lambda i, j: (i, j, 0))
    assert tm % (2 * SUBLANES) == 0 and tm // (2 * SUBLANES) >= FFN_PARTS
    scratch = [pltpu.VMEM((FFN_PARTS * 2 * SUBLANES, 2 * D_FF), F32), pltpu.VMEM((tm, D_FF), BF16),
               pltpu.VMEM((d // LANES, tm, LANES), F32)]
    if last:
        scratch += [pltpu.VMEM((2, tm, d), F32), pltpu.SemaphoreType.DMA((2,))]
    return pl.pallas_call(
        functools.partial(_ffn_last_kernel if last else _ffn_kernel, tm),
        grid=(b, lp // tm),
        in_specs=[row(D_MIX), row(d)] + [_const_spec(a.shape) if a.ndim == 2 else _layer_spec(a, l)
                                         for a in (wout, pmn, pfn, up, cw, cb, down, pon)],
        out_specs=pl.BlockSpec(memory_space=pl.ANY) if last else row(d),
        out_shape=jax.ShapeDtypeStruct((b, lp - CH, d) if last else (b, lp, d), F32),
        scratch_shapes=scratch,
        compiler_params=pltpu.CompilerParams(
            dimension_semantics=("arbitrary", "arbitrary") if last else ("parallel", "arbitrary"),
            vmem_limit_bytes=VMEM_LIMIT),
        name="ffn_last" if last else "ffn",
    )(merged, h, wout, pmn, pfn, up, cw, cb, down, pon)


def _rope_tables(lp):
    half = RET_DK // 2
    inv = ROPE_BASE ** (-np.arange(half, dtype=np.float64) / half)
    ang = (np.arange(lp, dtype=np.float64) - PAD)[:, None] * inv[None, :]
    c, s = np.cos(ang), np.sin(ang)
    return (jnp.asarray(np.concatenate([c, c], axis=-1), F32),
            jnp.asarray(np.concatenate([-s, s], axis=-1), F32))


def _row_tile(lp):
    for tm in (640, 512, 384, 256, 128):
        if lp % tm == 0:
            return tm
    raise ValueError(f"padded length {lp} is not a multiple of {CH}")


def kernel(x, meta_tokens, pre_mix_norm, w_in, gla_gate_w2, gla_gate_b, ret_norm_w, gla_norm_w, w_out,
           post_mix_norm, pre_ffn_norm, ffn_up, ffn_conv_w, ffn_conv_b, ffn_down, post_ffn_norm):
    bsz, seq, d = x.shape
    depth = w_in.shape[0]
    assert d == D_MODEL and seq % CH == 0
    lp = CH + seq
    tm = _row_tile(lp)

    cos2, sin2 = _rope_tables(lp)
    consts = _mixer_constants(tm)

    vec = lambda v: v.reshape(depth, 1, -1).astype(F32)
    w_b = w_in.astype(BF16)
    w_ga = w_b[:, :, MAIN_W:]
    w_ga3 = jnp.concatenate(
        [w_ga, w_ga, w_ga, jnp.zeros((depth, d, LANES - 3 * GLA_GATE_RANK), BF16)], axis=2)
    w2 = gla_gate_w2.astype(F32)
    w2_hi = w2.astype(BF16)
    w2_lo = (w2 - w2_hi.astype(F32)).astype(BF16)
    w2cat = jnp.concatenate(
        [w2_hi, w2_hi, w2_lo, jnp.zeros((depth, LANES - 3 * GLA_GATE_RANK, GLA_QK), BF16)], axis=1)
    pre_mix, gate_b, ret_nw, gla_nw = vec(pre_mix_norm), vec(gla_gate_b), vec(ret_norm_w), vec(gla_norm_w)
    ffn_w = (w_out.astype(F32), ffn_up.astype(F32), ffn_down.astype(F32))
    post_mix, pre_ffn, post_ffn = vec(post_mix_norm), vec(pre_ffn_norm), vec(post_ffn_norm)
    conv_w, conv_b = ffn_conv_w.astype(F32), vec(ffn_conv_b)
    h = x.astype(F32)
    for l in range(depth):
        outs = _proj(h, meta_tokens.astype(F32) if l == 0 else None, pre_mix, w_b, w_ga3, w2cat, gate_b,
                     cos2, sin2, consts["zeta_t"], ffn_w, l, tm)
        if l == 0:
            h = outs[-1]
        wo_b, up_b, dn_b = outs[5:8]
        merged = _mixer(*outs[:5], ret_nw, gla_nw, consts, l)
        h = _ffn(merged, h, wo_b, post_mix, pre_ffn, up_b, conv_w, conv_b, dn_b, post_ffn, l, tm,
                 last=(l == depth - 1))
    return h
```
